```python
import math
import jax, jax.numpy as jnp
from jax import lax
import numpy as np

D_MODEL = 2048
BATCH = 4
SEQ = 2048
DEPTH = 4

HEAD_DIM = 64
BLOCK = 128
EPS = 1e-6
H_A = D_MODEL // (2 * HEAD_DIM)
Q_LORA = D_MODEL // 4
KV_LORA = D_MODEL // 8
IDX_HEADS = 16
IDX_DIM = 64
TOPK_MAX = 256
TOPK_DIV = 4
H_B = D_MODEL // (2 * HEAD_DIM)
KV_B = H_B // 8
WINDOW = 128
H_C = D_MODEL // HEAD_DIM
N_BUCKETS = 32
MAX_DISTANCE = 1024
D_FF = 128 * (-(-(8 * D_MODEL // 3) // 128))
CONV_WIDTH = 3
PLE_DIM = 256

W_IN_EVEN = Q_LORA + KV_LORA + IDX_DIM + IDX_HEADS + H_B * HEAD_DIM + 2 * KV_B * HEAD_DIM
W_IN_ODD = 3 * H_C * HEAD_DIM + H_C
N_EVEN = (DEPTH + 1) // 2
N_ODD = DEPTH // 2

kernel_name = "hybrid_dsa_swa_fox_convffn_trunk"


def rms_norm(x, g):
    xf = x.astype(jnp.float32)
    y = xf * lax.rsqrt(jnp.mean(xf * xf, axis=-1, keepdims=True) + EPS)
    return y.astype(x.dtype) * g


def rel_bucket(dist):
    n = jnp.maximum(dist, 0)
    exact = N_BUCKETS // 2
    nf = jnp.maximum(n, 1).astype(jnp.float32)
    large = exact + (jnp.log(nf / exact) / math.log(MAX_DISTANCE / exact)
                     * (N_BUCKETS - exact)).astype(jnp.int32)
    large = jnp.minimum(large, N_BUCKETS - 1)
    return jnp.where(n < exact, n, large)


def to_blocks(t):
    b, s = t.shape[:2]
    return jnp.moveaxis(t.reshape((b, s // BLOCK, BLOCK) + t.shape[2:]), 1, 0)


def from_blocks(o):
    nb, b, q, f = o.shape
    return jnp.moveaxis(o, 0, 1).reshape(b, nb * q, f)


def dsa_attention(c_q, c_kv, k_idx, w_idx, cq_g, ckv_g, w_uq, q_g, w_qidx, w_uv, bias_tab):
    B, S, _ = c_q.shape
    nb = S // BLOCK
    topk = min(TOPK_MAX, S // TOPK_DIV)
    cq = rms_norm(c_q, cq_g)
    kv = rms_norm(c_kv, ckv_g)
    q = rms_norm((cq @ w_uq).reshape(B, S, H_A, KV_LORA), q_g)
    q_idx = (cq @ w_qidx).reshape(B, S, IDX_HEADS, IDX_DIM)
    idx_scale = (IDX_DIM * IDX_HEADS) ** -0.5
    att_scale = KV_LORA ** -0.5
    pos = jnp.arange(S)

    def block(args):
        n, q_b, qi_b, wi_b = args
        t = n * BLOCK + jnp.arange(BLOCK)
        dots = jax.nn.relu(jnp.einsum('bqhd,bsd->bqhs', qi_b, k_idx))
        score = jnp.einsum('bqh,bqhs->bqs', wi_b, dots).astype(jnp.float32) * idx_scale
        score = jnp.where(pos[None, None, :] <= t[None, :, None], score, -jnp.inf)
        _, sel = lax.top_k(score, topk)
        kv_sel = jax.vmap(lambda kvb, ib: kvb[ib])(kv, sel)
        dist = t[None, :, None] - sel
        bias = jnp.moveaxis(bias_tab[rel_bucket(dist)], 3, 1)
        logits = (jnp.einsum('bqhc,bqkc->bhqk', q_b, kv_sel).astype(jnp.float32) * att_scale
                  + bias.astype(jnp.float32))
        logits = jnp.where((dist >= 0)[:, None], logits, -jnp.inf)
        probs = jax.nn.softmax(logits, axis=-1).astype(kv.dtype)
        o_lat = jnp.einsum('bhqk,bqkc->bqhc', probs, kv_sel)
        o = jnp.einsum('bqhc,hcd->bqhd', o_lat, w_uv)
        return o.reshape(B, BLOCK, H_A * HEAD_DIM)

    out = lax.map(block, (jnp.arange(nb), to_blocks(q), to_blocks(q_idx), to_blocks(w_idx)))
    return from_blocks(out)


def swa_sink_attention(q, k, v, q_g, k_g, sinks, bias_tab):
    B, S, _ = q.shape
    nb = S // BLOCK
    G = H_B // KV_B
    q = rms_norm(q.reshape(B, S, H_B, HEAD_DIM), q_g).reshape(B, nb, BLOCK, KV_B, G, HEAD_DIM)
    k = rms_norm(k.reshape(B, S, KV_B, HEAD_DIM), k_g)
    v = v.reshape(B, S, KV_B, HEAD_DIM)

    def band(t):
        tb = jnp.pad(t, ((0, 0), (BLOCK, 0), (0, 0), (0, 0))).reshape(B, nb + 1, BLOCK, KV_B, HEAD_DIM)
        return jnp.concatenate([tb[:, :-1], tb[:, 1:]], axis=2)

    kb, vb = band(k), band(v)
    i = jnp.arange(BLOCK)[:, None]
    j = jnp.arange(2 * BLOCK)[None, :]
    dist = i + BLOCK - j
    key_pos = jnp.arange(nb)[:, None, None] * BLOCK - BLOCK + j[None]
    mask = (dist >= 0) & (dist < WINDOW) & (key_pos >= 0)
    bias = jnp.moveaxis(bias_tab[rel_bucket(dist)], -1, 0).reshape(KV_B, G, BLOCK, 2 * BLOCK)
    logits = (jnp.einsum('bnqhgd,bnkhd->bnhgqk', q, kb).astype(jnp.float32) * HEAD_DIM ** -0.5
              + bias.astype(jnp.float32))
    logits = jnp.where(mask[None, :, None, None], logits, -jnp.inf)
    sink = jnp.broadcast_to(sinks.reshape(KV_B, G, 1, 1).astype(jnp.float32), logits.shape[:-1] + (1,))
    probs = jax.nn.softmax(jnp.concatenate([logits, sink], axis=-1), axis=-1)[..., :-1].astype(v.dtype)
    o = jnp.einsum('bnhgqk,bnkhd->bnqhgd', probs, vb)
    return o.reshape(B, S, H_B * HEAD_DIM)


def forgetting_attention(proj, f_bias, q_g, k_g):
    B, S, _ = proj.shape
    nb = S // BLOCK
    HD = H_C * HEAD_DIM
    q, k, v, f = jnp.split(proj, [HD, 2 * HD, 3 * HD], axis=-1)
    q = rms_norm(q.reshape(B, S, H_C, HEAD_DIM), q_g)
    k = rms_norm(k.reshape(B, S, H_C, HEAD_DIM), k_g)
    v = v.reshape(B, S, H_C, HEAD_DIM)
    log_f = jax.nn.log_sigmoid((f + f_bias).astype(jnp.float32))
    cum = jnp.cumsum(log_f, axis=1)
    cum_k = jnp.moveaxis(cum, 2, 1)
    pos = jnp.arange(S)
    scale = HEAD_DIM ** -0.5

    def block(args):
        n, q_b, c_b = args
        t = n * BLOCK + jnp.arange(BLOCK)
        decay = jnp.moveaxis(c_b, 2, 1)[..., None] - cum_k[:, :, None, :]
        logits = jnp.einsum('bqhd,bshd->bhqs', q_b, k).astype(jnp.float32) * scale + decay
        logits = jnp.where(pos[None, :] <= t[:, None], logits, -jnp.inf)
        probs = jax.nn.softmax(logits, axis=-1).astype(v.dtype)
        return jnp.einsum('bhqs,bshd->bqhd', probs, v).reshape(B, BLOCK, HD)

    out = lax.map(block, (jnp.arange(nb), to_blocks(q), to_blocks(cum)))
    return from_blocks(out)


def conv_ffn(h, w_up, conv_w, w_down):
    u = h @ w_up
    C = u.shape[-1]
    u = lax.conv_general_dilated(u, conv_w[:, None, :], window_strides=(1,),
                                 padding=[(CONV_WIDTH - 1, 0)],
                                 dimension_numbers=('NWC', 'WIO', 'NWC'),
                                 feature_group_count=C)
    gate, up = jnp.split(u, 2, axis=-1)
    return (jax.nn.silu(gate) * up) @ w_down


def setup_inputs(seed: int = 0) -> dict:
    key = jax.random.key(seed)
    ks = iter(jax.random.split(key, 40))
    f32 = jnp.float32
    D = D_MODEL

    def w(shape, fan_in):
        return jax.random.normal(next(ks), shape, f32) * fan_in ** -0.5

    def gain(shape):
        return 1.0 + 0.02 * jax.random.normal(next(ks), shape, f32)

    def normal(shape, s):
        return s * jax.random.normal(next(ks), shape, f32)

    return {
        "x": jax.random.normal(next(ks), (BATCH, SEQ, D), f32),
        "p": jax.random.normal(next(ks), (DEPTH, BATCH, SEQ, PLE_DIM), f32),
        "attn_norm": gain((DEPTH, D)),
        "ffn_norm": gain((DEPTH, D)),
        "ple_norm": gain((DEPTH, D)),
        "rel_bias": normal((N_BUCKETS, H_A + H_B), 0.3),
        "w_in_even": w((N_EVEN, D, W_IN_EVEN), D),
        "a_cq_norm": gain((N_EVEN, Q_LORA)),
        "a_ckv_norm": gain((N_EVEN, KV_LORA)),
        "a_w_uq": w((N_EVEN, Q_LORA, H_A * KV_LORA), Q_LORA),
        "a_q_norm": gain((N_EVEN, KV_LORA)),
        "a_w_qidx": w((N_EVEN, Q_LORA, IDX_HEADS * IDX_DIM), Q_LORA),
        "a_w_uv": w((N_EVEN, H_A, KV_LORA, HEAD_DIM), KV_LORA),
        "b_q_norm": gain((N_EVEN, HEAD_DIM)),
        "b_k_norm": gain((N_EVEN, HEAD_DIM)),
        "b_sinks": normal((N_EVEN, H_B), 1.0),
        "w_out_even": w((N_EVEN, D, D), D),
        "w_in_odd": w((N_ODD, D, W_IN_ODD), D),
        "c_forget_bias": 2.5 + normal((N_ODD, H_C), 1.0),
        "c_q_norm": gain((N_ODD, HEAD_DIM)),
        "c_k_norm": gain((N_ODD, HEAD_DIM)),
        "w_out_odd": w((N_ODD, D, D), D),
        "w_up": w((DEPTH, D, 2 * D_FF), D),
        "ffn_conv": w((DEPTH, CONV_WIDTH, 2 * D_FF), CONV_WIDTH),
        "w_down": w((DEPTH, D_FF, D), D_FF),
        "w_ple_gate": w((DEPTH, D, D), D),
        "w_ple_proj": w((DEPTH, PLE_DIM, D), PLE_DIM),
    }


def reference(x, p, attn_norm, ffn_norm, ple_norm, rel_bias, w_in_even, a_cq_norm, a_ckv_norm,
              a_w_uq, a_q_norm, a_w_qidx, a_w_uv, b_q_norm, b_k_norm, b_sinks, w_out_even,
              w_in_odd, c_forget_bias, c_q_norm, c_k_norm, w_out_odd, w_up, ffn_conv, w_down,
              w_ple_gate, w_ple_proj):
    bias_a = rel_bias[:, :H_A]
    bias_b = rel_bias[:, H_A:]
    o1 = Q_LORA
    o2 = o1 + KV_LORA
    o3 = o2 + IDX_DIM
    o4 = o3 + IDX_HEADS
    o5 = o4 + H_B * HEAD_DIM
    o6 = o5 + KV_B * HEAD_DIM
    for i in range(DEPTH):
        h = rms_norm(x, attn_norm[i])
        if i % 2 == 0:
            e = i // 2
            proj = h @ w_in_even[e]
            c_q, c_kv, k_idx, w_idx, qb, kb, vb = jnp.split(proj, [o1, o2, o3, o4, o5, o6], axis=-1)
            y_a = dsa_attention(c_q, c_kv, k_idx, w_idx, a_cq_norm[e], a_ckv_norm[e], a_w_uq[e],
                                a_q_norm[e], a_w_qidx[e], a_w_uv[e], bias_a)
            y_b = swa_sink_attention(qb, kb, vb, b_q_norm[e], b_k_norm[e], b_sinks[e], bias_b)
            y = jnp.concatenate([y_a, y_b], axis=-1) @ w_out_even[e]
        else:
            o = i // 2
            y = forgetting_attention(h @ w_in_odd[o], c_forget_bias[o], c_q_norm[o], c_k_norm[o]) @ w_out_odd[o]
        x = x + y
        x = x + conv_ffn(rms_norm(x, ffn_norm[i]), w_up[i], ffn_conv[i], w_down[i])
        gate = jax.nn.sigmoid(rms_norm(x, ple_norm[i]) @ w_ple_gate[i])
        x = x + gate * (p[i] @ w_ple_proj[i])
    return x
```

```python
import functools
import math

import jax
import jax.numpy as jnp
from jax import lax
from jax.experimental import pallas as pl
from jax.experimental.pallas import tpu as pltpu

HEAD_DIM = 64
BLOCK = 128
EPS = 1e-6
H_A = 16
Q_LORA = 512
KV_LORA = 256
IDX_HEADS = 16
IDX_DIM = 64
TOPK_MAX = 256
TOPK_DIV = 4
H_B = 16
KV_B = 2
WINDOW = 128
H_C = 32
N_BUCKETS = 32
MAX_DISTANCE = 1024
CONV_WIDTH = 3

LANE = 128
NEG = -1e30
KEY_TILE = 256
HALO = 16
VMEM_LIMIT = 56 * 1024 * 1024
INT_MIN = -2 ** 31

_NT = (((1,), (1,)), ((), ()))


def _cp(sem, vmem=VMEM_LIMIT):
    return pltpu.CompilerParams(dimension_semantics=sem, vmem_limit_bytes=vmem)


def _rms(x, g):
    return x * lax.rsqrt(jnp.mean(x * x, axis=-1, keepdims=True) + EPS) * g


def _pair_norm(x, g2):
    lo = lax.broadcasted_iota(jnp.int32, x.shape, 1) < HEAD_DIM
    x2 = x * x
    s_lo = jnp.sum(jnp.where(lo, x2, 0.0), axis=-1, keepdims=True)
    s_hi = jnp.sum(jnp.where(lo, 0.0, x2), axis=-1, keepdims=True)
    inv = jnp.where(lo, lax.rsqrt(s_lo / HEAD_DIM + EPS), lax.rsqrt(s_hi / HEAD_DIM + EPS))
    return x * inv * g2


def _norm_matmul_kernel(x_ref, g_ref, w_ref, o_ref, h_ref):
    @pl.when(pl.program_id(1) == 0)
    def _():
        h_ref[...] = _rms(x_ref[...], g_ref[...]).astype(h_ref.dtype)

    o_ref[...] = jnp.dot(h_ref[...], w_ref[...], preferred_element_type=jnp.float32)


def norm_matmul(x, g, w, *, tm, tn):
    m, d = x.shape
    n = w.shape[1]
    return pl.pallas_call(
        _norm_matmul_kernel,
        out_shape=jax.ShapeDtypeStruct((m, n), jnp.float32),
        grid=(m // tm, n // tn),
        in_specs=[
            pl.BlockSpec((tm, d), lambda i, j: (i, 0)),
            pl.BlockSpec((1, d), lambda i, j: (0, 0)),
            pl.BlockSpec((d, tn), lambda i, j: (0, j)),
        ],
        out_specs=pl.BlockSpec((tm, tn), lambda i, j: (i, j)),
        scratch_shapes=[pltpu.VMEM((tm, d), jnp.bfloat16)],
        compiler_params=_cp(("parallel", "arbitrary")),
        name="norm_matmul",
    )(x, g.reshape(1, d), w)


def _matmul_res_kernel(n_pairs, *refs):
    x_ref = refs[2 * n_pairs]
    o_ref = refs[2 * n_pairs + 1]
    acc = x_ref[...]
    for p in range(n_pairs):
        acc = acc + jnp.dot(refs[2 * p][...], refs[2 * p + 1][...], preferred_element_type=jnp.float32)
    o_ref[...] = acc


def matmul_res(pairs, x, *, tm, tn):
    m, n = x.shape
    in_specs, args = [], []
    for a, w in pairs:
        k = a.shape[1]
        in_specs += [pl.BlockSpec((tm, k), lambda i, j: (i, 0)), pl.BlockSpec((k, tn), lambda i, j: (0, j))]
        args += [a, w]
    in_specs.append(pl.BlockSpec((tm, tn), lambda i, j: (i, j)))
    return pl.pallas_call(
        functools.partial(_matmul_res_kernel, len(pairs)),
        out_shape=jax.ShapeDtypeStruct((m, n), jnp.float32),
        grid=(m // tm, n // tn),
        in_specs=in_specs,
        out_specs=pl.BlockSpec((tm, tn), lambda i, j: (i, j)),
        compiler_params=_cp(("parallel", "parallel")),
        name="matmul_res",
    )(*args, x)


def _ffn_kernel(tiles_per_seq, x_ref, xh_ref, g_ref, wg_ref, wu_ref, cg_ref, cu_ref, wd_ref, o_ref, h_ref, acc_ref):
    i = pl.program_id(0)
    f = pl.program_id(1)
    tm = x_ref.shape[0]

    @pl.when(f == 0)
    def _():
        seq_start = (i % tiles_per_seq) == 0
        hh = _rms(xh_ref[...], g_ref[...])
        h_ref[0:HALO, :] = jnp.where(seq_start, 0.0, hh).astype(h_ref.dtype)
        h_ref[HALO:, :] = _rms(x_ref[...], g_ref[...]).astype(h_ref.dtype)
        acc_ref[...] = jnp.zeros_like(acc_ref)

    h = h_ref[...]

    def conv(z, c_ref):
        z1 = pltpu.roll(z, 1, axis=0)
        z2 = pltpu.roll(z, 2, axis=0)
        y = c_ref[2:3, :] * z + c_ref[1:2, :] * z1 + c_ref[0:1, :] * z2
        return y[HALO:, :]

    gate = conv(jnp.dot(h, wg_ref[...], preferred_element_type=jnp.float32), cg_ref)
    up = conv(jnp.dot(h, wu_ref[...], preferred_element_type=jnp.float32), cu_ref)
    a = (gate * jax.nn.sigmoid(gate) * up).astype(jnp.bfloat16)
    acc_ref[...] += jnp.dot(a, wd_ref[...], preferred_element_type=jnp.float32)

    @pl.when(f == pl.num_programs(1) - 1)
    def _():
        o_ref[...] = x_ref[...] + acc_ref[...]


def conv_ffn(x, g, wg, wu, cg, cu, wd, *, seq, tm, tf):
    m, d = x.shape
    fp = wg.shape[1]
    hb = tm // HALO
    return pl.pallas_call(
        functools.partial(_ffn_kernel, seq // tm),
        out_shape=jax.ShapeDtypeStruct((m, d), jnp.float32),
        grid=(m // tm, fp // tf),
        in_specs=[
            pl.BlockSpec((tm, d), lambda i, f: (i, 0)),
            pl.BlockSpec((HALO, d), lambda i, f: (jnp.maximum(i * hb - 1, 0), 0)),
            pl.BlockSpec((1, d), lambda i, f: (0, 0)),
            pl.BlockSpec((d, tf), lambda i, f: (0, f)),
            pl.BlockSpec((d, tf), lambda i, f: (0, f)),
            pl.BlockSpec((CONV_WIDTH, tf), lambda i, f: (0, f)),
            pl.BlockSpec((CONV_WIDTH, tf), lambda i, f: (0, f)),
            pl.BlockSpec((tf, d), lambda i, f: (f, 0)),
        ],
        out_specs=pl.BlockSpec((tm, d), lambda i, f: (i, 0)),
        scratch_shapes=[pltpu.VMEM((tm + HALO, d), jnp.bfloat16), pltpu.VMEM((tm, d), jnp.float32)],
        compiler_params=_cp(("parallel", "arbitrary")),
        name="conv_ffn",
    )(x, x, g.reshape(1, d), wg, wu, cg, cu, wd)


def _ple_kernel(x_ref, g_ref, p_ref, wg_ref, wp_ref, o_ref, h_ref):
    j = pl.program_id(1)
    tn = o_ref.shape[1]

    @pl.when(j == 0)
    def _():
        h_ref[...] = _rms(x_ref[...], g_ref[...]).astype(h_ref.dtype)

    gate = jax.nn.sigmoid(jnp.dot(h_ref[...], wg_ref[...], preferred_element_type=jnp.float32))
    pp = jnp.dot(p_ref[...].astype(jnp.bfloat16), wp_ref[...], preferred_element_type=jnp.float32)
    o_ref[...] = x_ref[:, pl.ds(pl.multiple_of(j * tn, LANE), tn)] + gate * pp


def ple(x, g, p, wg, wp, *, tm, tn):
    m, d = x.shape
    pd = p.shape[1]
    return pl.pallas_call(
        _ple_kernel,
        out_shape=jax.ShapeDtypeStruct((m, d), jnp.float32),
        grid=(m // tm, d // tn),
        in_specs=[
            pl.BlockSpec((tm, d), lambda i, j: (i, 0)),
            pl.BlockSpec((1, d), lambda i, j: (0, 0)),
            pl.BlockSpec((tm, pd), lambda i, j: (i, 0)),
            pl.BlockSpec((d, tn), lambda i, j: (0, j)),
            pl.BlockSpec((pd, tn), lambda i, j: (0, j)),
        ],
        out_specs=pl.BlockSpec((tm, tn), lambda i, j: (i, j)),
        scratch_shapes=[pltpu.VMEM((tm, d), jnp.bfloat16)],
        compiler_params=_cp(("parallel", "arbitrary")),
        name="ple",
    )(x, g.reshape(1, d), p, wg, wp)


def _dsa_prep_kernel(cq_ref, ckv_ref, gcq_ref, gckv_ref, gq_ref, wuq_ref, wqi_ref, q_ref, qi_ref, kv_ref):
    tm = cq_ref.shape[0]
    cq = _rms(cq_ref[...], gcq_ref[...]).astype(jnp.bfloat16)
    kv_ref[...] = _rms(ckv_ref[...], gckv_ref[...]).astype(kv_ref.dtype)
    qi_ref[...] = jnp.dot(cq, wqi_ref[...], preferred_element_type=jnp.float32).astype(qi_ref.dtype)
    att_scale = KV_LORA ** -0.5
    for h in range(H_A):
        ql = jnp.dot(cq, wuq_ref[:, h * KV_LORA:(h + 1) * KV_LORA], preferred_element_type=jnp.float32)
        qn = (_rms(ql, gq_ref[...]) * att_scale).astype(q_ref.dtype)
        for r in range(tm // BLOCK):
            q_ref[(r * H_A + h) * BLOCK:(r * H_A + h + 1) * BLOCK, :] = qn[r * BLOCK:(r + 1) * BLOCK, :]


def dsa_prep(proj, gcq, gckv, gq, wuq, wqi, *, tm):
    m = proj.shape[0]
    cq_blk = 1024 // Q_LORA
    ckv_blk = 1536 // KV_LORA
    return pl.pallas_call(
        _dsa_prep_kernel,
        out_shape=(
            jax.ShapeDtypeStruct((m * H_A, KV_LORA), jnp.bfloat16),
            jax.ShapeDtypeStruct((m, IDX_HEADS * IDX_DIM), jnp.bfloat16),
            jax.ShapeDtypeStruct((m, KV_LORA), jnp.bfloat16),
        ),
        grid=(m // tm,),
        in_specs=[
            pl.BlockSpec((tm, Q_LORA), lambda i: (i, cq_blk)),
            pl.BlockSpec((tm, KV_LORA), lambda i: (i, ckv_blk)),
            pl.BlockSpec((1, Q_LORA), lambda i: (0, 0)),
            pl.BlockSpec((1, KV_LORA), lambda i: (0, 0)),
            pl.BlockSpec((1, KV_LORA), lambda i: (0, 0)),
            pl.BlockSpec((Q_LORA, H_A * KV_LORA), lambda i: (0, 0)),
            pl.BlockSpec((Q_LORA, IDX_HEADS * IDX_DIM), lambda i: (0, 0)),
        ],
        out_specs=(
            pl.BlockSpec((tm * H_A, KV_LORA), lambda i: (i, 0)),
            pl.BlockSpec((tm, IDX_HEADS * IDX_DIM), lambda i: (i, 0)),
            pl.BlockSpec((tm, KV_LORA), lambda i: (i, 0)),
        ),
        compiler_params=_cp(("parallel",)),
        name="dsa_prep",
    )(proj, proj, gcq.reshape(1, -1), gckv.reshape(1, -1), gq.reshape(1, -1), wuq, wqi)


def _dsa_attn_kernel(topk, g_off, q_ref, qi_ref, kwq_ref, kwall_ref, kv_ref, g_ref, wuv_ref, o_ref,
                     ka_ref, kb_ref, key_ref, m_ref, l_ref, acc_ref):
    n = pl.program_id(1)
    n_tiles = (n + 2) // 2
    t_pos = n * BLOCK + lax.broadcasted_iota(jnp.int32, (BLOCK, KEY_TILE), 0)
    lane = lax.broadcasted_iota(jnp.int32, (BLOCK, KEY_TILE), 1)

    @pl.when(n == 0)
    def _():
        kw = kwall_ref[...]
        lo = lax.broadcasted_iota(jnp.int32, kw.shape, 1) < IDX_DIM
        ka_ref[...] = jnp.where(lo, kw, 0.0).astype(ka_ref.dtype)
        kb_ref[...] = jnp.where(lo, 0.0, pltpu.roll(kw, IDX_DIM, axis=1)).astype(kb_ref.dtype)

    w_idx = kwq_ref[:, IDX_DIM:IDX_DIM + IDX_HEADS]

    def score_tile(j, carry):
        col = pl.multiple_of(j * KEY_TILE, KEY_TILE)
        ka = ka_ref[pl.ds(col, KEY_TILE), :]
        kb = kb_ref[pl.ds(col, KEY_TILE), :]
        s = jnp.zeros((BLOCK, KEY_TILE), jnp.float32)
        for p in range(IDX_HEADS // 2):
            qp = qi_ref[:, p * LANE:(p + 1) * LANE]
            da = lax.dot_general(qp, ka, _NT, preferred_element_type=jnp.float32)
            db = lax.dot_general(qp, kb, _NT, preferred_element_type=jnp.float32)
            s = s + w_idx[:, 2 * p:2 * p + 1] * jnp.maximum(da, 0.0)
            s = s + w_idx[:, 2 * p + 1:2 * p + 2] * jnp.maximum(db, 0.0)
        s = jnp.where(col + lane <= t_pos, s, -jnp.inf)
        bits = pltpu.bitcast(s, jnp.int32)
        key_ref[:, pl.ds(col, KEY_TILE)] = jnp.where(bits < 0, bits ^ jnp.int32(0x7FFFFFFF), bits)
        return carry

    lax.fori_loop(0, n_tiles, score_tile, 0)

    def bit_step(b, tu):
        cand_u = tu | lax.shift_left(jnp.int32(1), jnp.int32(31) - b)
        cand_s = cand_u ^ jnp.int32(INT_MIN)

        def count_tile(j, cnt):
            col = pl.multiple_of(j * KEY_TILE, KEY_TILE)
            return cnt + jnp.where(key_ref[:, pl.ds(col, KEY_TILE)] >= cand_s, 1, 0)

        cnt = lax.fori_loop(0, n_tiles, count_tile, jnp.zeros((BLOCK, KEY_TILE), jnp.int32))
        total = jnp.sum(cnt, axis=-1, keepdims=True)
        return jnp.where(total >= topk, cand_u, tu)

    thr = lax.fori_loop(0, 32, bit_step, jnp.zeros((BLOCK, 1), jnp.int32)) ^ jnp.int32(INT_MIN)

    m_ref[...] = jnp.full_like(m_ref, NEG)
    l_ref[...] = jnp.zeros_like(l_ref)
    acc_ref[...] = jnp.zeros_like(acc_ref)

    def attn_tile(j, carry):
        col = pl.multiple_of(j * KEY_TILE, KEY_TILE)
        sel = (key_ref[:, pl.ds(col, KEY_TILE)] >= thr) & (col + lane <= t_pos)
        am = jnp.where(sel, 0.0, NEG)
        kv = kv_ref[pl.ds(col, KEY_TILE), :]
        s = lax.dot_general(q_ref[...], kv, _NT, preferred_element_type=jnp.float32)
        gcol = pl.multiple_of(g_off - n * BLOCK + col, LANE)
        s = s.reshape(H_A, BLOCK, KEY_TILE) + (g_ref[:, :, pl.ds(gcol, KEY_TILE)] + am[None])
        s = s.reshape(H_A * BLOCK, KEY_TILE)
        m_prev = m_ref[...]
        m_new = jnp.maximum(m_prev, jnp.max(s, axis=-1, keepdims=True))
        alpha = jnp.exp(m_prev - m_new)
        p = jnp.exp(s - m_new)
        l_ref[...] = alpha * l_ref[...] + jnp.sum(p, axis=-1, keepdims=True)
        acc_ref[...] = alpha * acc_ref[...] + jnp.dot(p.astype(kv.dtype), kv, preferred_element_type=jnp.float32)
        m_ref[...] = m_new
        return carry

    lax.fori_loop(0, n_tiles, attn_tile, 0)

    o_lat = (acc_ref[...] / l_ref[...]).astype(jnp.bfloat16)
    for p in range(H_A // 2):
        pair = jnp.concatenate([o_lat[(2 * p) * BLOCK:(2 * p + 1) * BLOCK, :],
                                o_lat[(2 * p + 1) * BLOCK:(2 * p + 2) * BLOCK, :]], axis=1)
        o_ref[:, p * LANE:(p + 1) * LANE] = jnp.dot(
            pair, wuv_ref[p], preferred_element_type=jnp.float32).astype(o_ref.dtype)


def dsa_attn(q, qi, proj, kv, gtab, wuv2, *, batch, seq):
    nb = seq // BLOCK
    topk = min(TOPK_MAX, seq // TOPK_DIV)
    kw_blk = 1792 // LANE
    g_off = seq - BLOCK
    return pl.pallas_call(
        functools.partial(_dsa_attn_kernel, topk, g_off),
        out_shape=jax.ShapeDtypeStruct((batch * seq, H_A * HEAD_DIM), jnp.bfloat16),
        grid=(batch, nb),
        in_specs=[
            pl.BlockSpec((H_A * BLOCK, KV_LORA), lambda b, n: (b * nb + n, 0)),
            pl.BlockSpec((BLOCK, IDX_HEADS * IDX_DIM), lambda b, n: (b * nb + n, 0)),
            pl.BlockSpec((BLOCK, LANE), lambda b, n: (b * nb + n, kw_blk)),
            pl.BlockSpec((seq, LANE), lambda b, n: (b, kw_blk)),
            pl.BlockSpec((seq, KV_LORA), lambda b, n: (b, 0)),
            pl.BlockSpec(gtab.shape, lambda b, n: (0, 0, 0), pipeline_mode=pl.Buffered(1)),
            pl.BlockSpec(wuv2.shape, lambda b, n: (0, 0, 0)),
        ],
        out_specs=pl.BlockSpec((BLOCK, H_A * HEAD_DIM), lambda b, n: (b * nb + n, 0)),
        scratch_shapes=[
            pltpu.VMEM((seq, LANE), jnp.bfloat16),
            pltpu.VMEM((seq, LANE), jnp.bfloat16),
            pltpu.VMEM((BLOCK, seq), jnp.int32),
            pltpu.VMEM((H_A * BLOCK, 1), jnp.float32),
            pltpu.VMEM((H_A * BLOCK, 1), jnp.float32),
            pltpu.VMEM((H_A * BLOCK, KV_LORA), jnp.float32),
        ],
        compiler_params=_cp(("parallel", "arbitrary")),
        name="dsa_attn",
    )(q, qi, proj, proj, kv, gtab, wuv2)


def _swa_kernel(q_ref, kc_ref, kp_ref, vc_ref, vp_ref, gq_ref, gk_ref, bias_ref, sink_ref, o_ref):
    n = pl.program_id(1)
    g = H_B // KV_B
    scale = HEAD_DIM ** -0.5
    kn = _pair_norm(jnp.concatenate([kp_ref[...], kc_ref[...]], axis=0), gk_ref[...])
    k_nat = kn.astype(jnp.bfloat16)
    k_swap = pltpu.roll(kn, HEAD_DIM, axis=1).astype(jnp.bfloat16)
    v = jnp.concatenate([vp_ref[...], vc_ref[...]], axis=0).astype(jnp.bfloat16)
    col = lax.broadcasted_iota(jnp.int32, (BLOCK, 2 * BLOCK), 1)
    first = jnp.where((n == 0) & (col < BLOCK), NEG, 0.0)
    lo = lax.broadcasted_iota(jnp.int32, (BLOCK, LANE), 1) < HEAD_DIM
    for p in range(H_B // 2):
        hk = (2 * p) // g
        qn = _pair_norm(q_ref[:, p * LANE:(p + 1) * LANE], gq_ref[...]) * scale
        halves = []
        for half in range(2):
            head = 2 * p + half
            qh = (jnp.where(lo, qn, 0.0) if half == 0 else jnp.where(lo, 0.0, qn)).astype(jnp.bfloat16)
            kh = k_nat if half == hk else k_swap
            s = lax.dot_general(qh, kh, _NT, preferred_element_type=jnp.float32) + bias_ref[head] + first
            sink = sink_ref[head]
            mx = jnp.maximum(jnp.max(s, axis=-1, keepdims=True), sink)
            e = jnp.exp(s - mx)
            den = jnp.sum(e, axis=-1, keepdims=True) + jnp.exp(sink - mx)
            o = jnp.dot((e / den).astype(jnp.bfloat16), v, preferred_element_type=jnp.float32)
            halves.append(o if half == hk else pltpu.roll(o, HEAD_DIM, axis=1))
        o_ref[:, p * LANE:(p + 1) * LANE] = jnp.where(lo, halves[0], halves[1]).astype(o_ref.dtype)


def swa_attn(proj, gq, gk, bias_tab, sinks, *, batch, seq):
    nb = seq // BLOCK
    k_blk = 1920 // LANE
    v_blk = 2048 // LANE
    prev = lambda b, n: b * nb + jnp.maximum(n - 1, 0)
    return pl.pallas_call(
        _swa_kernel,
        out_shape=jax.ShapeDtypeStruct((batch * seq, H_B * HEAD_DIM), jnp.bfloat16),
        grid=(batch, nb),
        in_specs=[
            pl.BlockSpec((BLOCK, H_B * HEAD_DIM), lambda b, n: (b * nb + n, 0)),
            pl.BlockSpec((BLOCK, LANE), lambda b, n: (b * nb + n, k_blk)),
            pl.BlockSpec((BLOCK, LANE), lambda b, n: (prev(b, n), k_blk)),
            pl.BlockSpec((BLOCK, LANE), lambda b, n: (b * nb + n, v_blk)),
            pl.BlockSpec((BLOCK, LANE), lambda b, n: (prev(b, n), v_blk)),
            pl.BlockSpec((1, LANE), lambda b, n: (0, 0)),
            pl.BlockSpec((1, LANE), lambda b, n: (0, 0)),
            pl.BlockSpec(bias_tab.shape, lambda b, n: (0, 0, 0)),
            pl.BlockSpec(memory_space=pltpu.SMEM),
        ],
        out_specs=pl.BlockSpec((BLOCK, H_B * HEAD_DIM), lambda b, n: (b * nb + n, 0)),
        compiler_params=_cp(("parallel", "parallel")),
        name="swa_attn",
    )(proj, proj, proj, proj, proj, jnp.tile(gq, 2).reshape(1, LANE), jnp.tile(gk, 2).reshape(1, LANE),
      bias_tab, sinks)


def _fox_prep_kernel(q_ref, k_ref, v_ref, f_ref, gq_ref, gk_ref, fb_ref, qo_ref, ko_ref, vo_ref, lf_ref):
    scale = HEAD_DIM ** -0.5
    for p in range(H_C // 2):
        sl = slice(p * LANE, (p + 1) * LANE)
        qo_ref[:, sl] = (_pair_norm(q_ref[:, sl], gq_ref[...]) * scale).astype(qo_ref.dtype)
        ko_ref[:, sl] = _pair_norm(k_ref[:, sl], gk_ref[...]).astype(ko_ref.dtype)
    vo_ref[...] = v_ref[...].astype(vo_ref.dtype)
    z = f_ref[...] + fb_ref[...]
    lf_ref[...] = jnp.minimum(z, 0.0) - jnp.log1p(jnp.exp(-jnp.abs(z)))


def fox_prep(proj, gq, gk, fbias, *, tm):
    m = proj.shape[0]
    hd = H_C * HEAD_DIM
    f_blk = 3 * hd // LANE
    fb = jnp.zeros((1, LANE), jnp.float32).at[0, :H_C].set(fbias)
    act = jax.ShapeDtypeStruct((m, hd), jnp.bfloat16)
    return pl.pallas_call(
        _fox_prep_kernel,
        out_shape=(act, act, act, jax.ShapeDtypeStruct((m, LANE), jnp.float32)),
        grid=(m // tm,),
        in_specs=[
            pl.BlockSpec((tm, hd), lambda i: (i, 0)),
            pl.BlockSpec((tm, hd), lambda i: (i, 1)),
            pl.BlockSpec((tm, hd), lambda i: (i, 2)),
            pl.BlockSpec((tm, LANE), lambda i: (i, f_blk)),
            pl.BlockSpec((1, LANE), lambda i: (0, 0)),
            pl.BlockSpec((1, LANE), lambda i: (0, 0)),
            pl.BlockSpec((1, LANE), lambda i: (0, 0)),
        ],
        out_specs=(
            pl.BlockSpec((tm, hd), lambda i: (i, 0)),
            pl.BlockSpec((tm, hd), lambda i: (i, 0)),
            pl.BlockSpec((tm, hd), lambda i: (i, 0)),
            pl.BlockSpec((tm, LANE), lambda i: (i, 0)),
        ),
        compiler_params=_cp(("parallel",)),
        name="fox_prep",
    )(proj, proj, proj, proj, jnp.tile(gq, 2).reshape(1, LANE), jnp.tile(gk, 2).reshape(1, LANE), fb)


def _cumsum_kernel(chunk, x_ref, o_ref):
    seq = x_ref.shape[0]
    r = lax.broadcasted_iota(jnp.int32, (chunk, chunk), 0)
    c = lax.broadcasted_iota(jnp.int32, (chunk, chunk), 1)
    tri = jnp.where(c <= r, 1.0, 0.0)
    carry = jnp.zeros((1, x_ref.shape[1]), jnp.float32)
    for s in range(seq // chunk):
        blk = x_ref[s * chunk:(s + 1) * chunk, :]
        cs = jnp.dot(tri, blk, preferred_element_type=jnp.float32, precision=lax.Precision.HIGHEST) + carry
        o_ref[s * chunk:(s + 1) * chunk, :] = cs
        carry = cs[chunk - 1:chunk, :]


def seq_cumsum(x, *, batch, seq):
    return pl.pallas_call(
        functools.partial(_cumsum_kernel, 256),
        out_shape=jax.ShapeDtypeStruct(x.shape, jnp.float32),
        grid=(batch,),
        in_specs=[pl.BlockSpec((seq, x.shape[1]), lambda b: (b, 0))],
        out_specs=pl.BlockSpec((seq, x.shape[1]), lambda b: (b, 0)),
        compiler_params=_cp(("parallel",)),
        name="seq_cumsum",
    )(x)


def _fox_attn_kernel(tq, tk, q_ref, k_ref, v_ref, cq_ref, ck_ref, o_ref, klo_ref, khi_ref, m_ref, l_ref, acc_ref):
    c = pl.program_id(1)
    i = pl.program_id(2)

    @pl.when(i == 0)
    def _():
        kf = k_ref[...].astype(jnp.float32)
        lo = lax.broadcasted_iota(jnp.int32, kf.shape, 1) < HEAD_DIM
        klo_ref[...] = jnp.where(lo, kf, 0.0).astype(klo_ref.dtype)
        khi_ref[...] = jnp.where(lo, 0.0, kf).astype(khi_ref.dtype)

    q = q_ref[...]
    cq_blk = cq_ref[...]
    cl = lax.broadcasted_iota(jnp.int32, cq_blk.shape, 1)
    cq = [jnp.sum(jnp.where(cl == 2 * c + hh, cq_blk, 0.0), axis=-1, keepdims=True) for hh in range(2)]
    krefs = (klo_ref, khi_ref)

    m_ref[...] = jnp.full_like(m_ref, NEG)
    l_ref[...] = jnp.zeros_like(l_ref)
    acc_ref[...] = jnp.zeros_like(acc_ref)

    def tile(j, masked):
        col = pl.multiple_of(j * tk, tk)
        v = v_ref[pl.ds(col, tk), :]
        for hh in range(2):
            kt = krefs[hh][pl.ds(col, tk), :]
            s = lax.dot_general(q, kt, _NT, preferred_element_type=jnp.float32)
            s = s + (cq[hh] - ck_ref[hh:hh + 1, pl.ds(col, tk)])
            if masked:
                rr = lax.broadcasted_iota(jnp.int32, s.shape, 0)
                cc = lax.broadcasted_iota(jnp.int32, s.shape, 1)
                s = jnp.where(cc <= rr, s, NEG)
            m_prev = m_ref[hh]
            m_new = jnp.maximum(m_prev, jnp.max(s, axis=-1, keepdims=True))
            alpha = jnp.exp(m_prev - m_new)
            p = jnp.exp(s - m_new)
            l_ref[hh] = alpha * l_ref[hh] + jnp.sum(p, axis=-1, keepdims=True)
            acc_ref[hh] = alpha * acc_ref[hh] + jnp.dot(p.astype(v.dtype), v, preferred_element_type=jnp.float32)
            m_ref[hh] = m_new

    def full_tile(j, carry):
        tile(j, False)
        return carry

    lax.fori_loop(0, i, full_tile, 0)
    tile(i, True)

    lane = lax.broadcasted_iota(jnp.int32, (tq, LANE), 1)
    out = jnp.where(lane < HEAD_DIM, acc_ref[0] / l_ref[0], acc_ref[1] / l_ref[1])
    o_ref[...] = out.astype(o_ref.dtype)


def fox_attn(q, k, v, cum, cum_t, *, batch, seq, tq):
    nq = seq // tq
    pairs = H_C // 2
    return pl.pallas_call(
        functools.partial(_fox_attn_kernel, tq, tq),
        out_shape=jax.ShapeDtypeStruct(q.shape, jnp.bfloat16),
        grid=(batch, pairs, nq),
        in_specs=[
            pl.BlockSpec((tq, LANE), lambda b, c, i: (b * nq + i, c)),
            pl.BlockSpec((seq, LANE), lambda b, c, i: (b, c)),
            pl.BlockSpec((seq, LANE), lambda b, c, i: (b, c)),
            pl.BlockSpec((tq, LANE), lambda b, c, i: (b * nq + i, 0)),
            pl.BlockSpec((None, None, 8, seq), lambda b, c, i: (b, c, 0, 0)),
        ],
        out_specs=pl.BlockSpec((tq, LANE), lambda b, c, i: (b * nq + i, c)),
        scratch_shapes=[
            pltpu.VMEM((seq, LANE), jnp.bfloat16),
            pltpu.VMEM((seq, LANE), jnp.bfloat16),
            pltpu.VMEM((2, tq, 1), jnp.float32),
            pltpu.VMEM((2, tq, 1), jnp.float32),
            pltpu.VMEM((2, tq, LANE), jnp.float32),
        ],
        compiler_params=_cp(("parallel", "parallel", "arbitrary")),
        name="fox_attn",
    )(q, k, v, cum, cum_t)


def _rel_bucket(dist):
    n = jnp.maximum(dist, 0)
    exact = N_BUCKETS // 2
    nf = jnp.maximum(n, 1).astype(jnp.float32)
    large = exact + (jnp.log(nf / exact) / math.log(MAX_DISTANCE / exact) * (N_BUCKETS - exact)).astype(jnp.int32)
    large = jnp.minimum(large, N_BUCKETS - 1)
    return jnp.where(n < exact, n, large)


def _pad_cols(w, n):
    return jnp.pad(w, ((0, 0), (0, n - w.shape[1])))


def kernel(x, p, attn_norm, ffn_norm, ple_norm, rel_bias, w_in_even, a_cq_norm, a_ckv_norm, a_w_uq, a_q_norm,
           a_w_qidx, a_w_uv, b_q_norm, b_k_norm, b_sinks, w_out_even, w_in_odd, c_forget_bias, c_q_norm,
           c_k_norm, w_out_odd, w_up, ffn_conv, w_down, w_ple_gate, w_ple_proj):
    batch, seq, d = x.shape
    depth = p.shape[0]
    m = batch * seq
    bf = jnp.bfloat16
    d_ff = w_down.shape[1]
    f_pad = -(-d_ff // 512) * 512
    hd_c = H_C * HEAD_DIM

    bias_a = rel_bias[:, :H_A]
    bias_b = rel_bias[:, H_A:]
    g_off = seq - BLOCK
    dist_a = jnp.arange(BLOCK)[:, None] + g_off - jnp.arange(g_off + KEY_TILE)[None, :]
    gtab = jnp.where((dist_a >= 0)[None], jnp.moveaxis(bias_a[_rel_bucket(dist_a)], -1, 0), 0.0)
    dist_b = jnp.arange(BLOCK)[:, None] + BLOCK - jnp.arange(2 * BLOCK)[None, :]
    band = (dist_b >= 0) & (dist_b < WINDOW)
    btab = jnp.where(band[None], jnp.moveaxis(bias_b[_rel_bucket(dist_b)], -1, 0), NEG)

    o1 = Q_LORA
    o2 = o1 + KV_LORA
    o3 = o2 + IDX_DIM
    o4 = o3 + IDX_HEADS
    o5 = o4 + H_B * HEAD_DIM
    o6 = o5 + KV_B * HEAD_DIM

    x = x.reshape(m, d)
    for i in range(depth):
        if i % 2 == 0:
            e = i // 2
            w = w_in_even[e]
            w_in = jnp.concatenate(
                [w[:, o4:o5], w[:, :o1], w[:, o1:o2], _pad_cols(w[:, o2:o4], LANE), w[:, o5:o6], w[:, o6:],
                 jnp.zeros((d, LANE), w.dtype)], axis=1).astype(bf)
            proj = norm_matmul(x, attn_norm[i], w_in, tm=1024, tn=w_in.shape[1] // 3)
            q, qi, kv = dsa_prep(proj, a_cq_norm[e], a_ckv_norm[e], a_q_norm[e], a_w_uq[e].astype(bf),
                                 a_w_qidx[e].astype(bf), tm=256)
            wuv = a_w_uv[e].reshape(H_A // 2, 2, KV_LORA, HEAD_DIM)
            zero = jnp.zeros_like(wuv[:, 0])
            wuv2 = jnp.concatenate([jnp.concatenate([wuv[:, 0], zero], axis=2),
                                    jnp.concatenate([zero, wuv[:, 1]], axis=2)], axis=1).astype(bf)
            y_a = dsa_attn(q, qi, proj, kv, gtab, wuv2, batch=batch, seq=seq)
            y_b = swa_attn(proj, b_q_norm[e], b_k_norm[e], btab, b_sinks[e], batch=batch, seq=seq)
            wo = w_out_even[e].astype(bf)
            x = matmul_res([(y_a, wo[:H_A * HEAD_DIM]), (y_b, wo[H_A * HEAD_DIM:])], x, tm=1024, tn=512)
        else:
            o = i // 2
            w_in = jnp.pad(w_in_odd[o], ((0, 0), (0, 3 * hd_c + LANE - w_in_odd[o].shape[1]))).astype(bf)
            proj = norm_matmul(x, attn_norm[i], w_in, tm=1024, tn=w_in.shape[1] // 7)
            qn, kn, vb, logf = fox_prep(proj, c_q_norm[o], c_k_norm[o], c_forget_bias[o], tm=256)
            cum = seq_cumsum(logf, batch=batch, seq=seq)
            cum_t = jnp.swapaxes(cum.reshape(batch, seq, LANE)[:, :, :H_C], 1, 2)
            cum_t = jnp.pad(cum_t.reshape(batch, H_C // 2, 2, seq), ((0, 0), (0, 0), (0, 6), (0, 0)))
            y = fox_attn(qn, kn, vb, cum, cum_t, batch=batch, seq=seq, tq=256)
            x = matmul_res([(y, w_out_odd[o].astype(bf))], x, tm=1024, tn=512)

        wg = _pad_cols(w_up[i][:, :d_ff], f_pad).astype(bf)
        wu = _pad_cols(w_up[i][:, d_ff:], f_pad).astype(bf)
        cg = _pad_cols(ffn_conv[i][:, :d_ff], f_pad)
        cu = _pad_cols(ffn_conv[i][:, d_ff:], f_pad)
        wd = jnp.pad(w_down[i], ((0, f_pad - d_ff), (0, 0))).astype(bf)
        x = conv_ffn(x, ffn_norm[i], wg, wu, cg, cu, wd, seq=seq, tm=512, tf=512)
        x = ple(x, ple_norm[i], p[i].reshape(m, -1), w_ple_gate[i].astype(bf), w_ple_proj[i].astype(bf),
                tm=1024, tn=512)
    return x.reshape(batch, seq, d)
```

```python
import functools
import math

import jax
import jax.numpy as jnp
import numpy as np
from jax import lax
from jax.experimental import pallas as pl
from jax.experimental.pallas import tpu as pltpu

HEAD_DIM = 64
BLOCK = 128
EPS = 1e-6
H_A = 16
Q_LORA = 512
KV_LORA = 256
IDX_HEADS = 16
IDX_DIM = 64
TOPK_MAX = 256
TOPK_DIV = 4
H_B = 16
KV_B = 2
WINDOW = 128
H_C = 32
N_BUCKETS = 32
MAX_DISTANCE = 1024
CONV_WIDTH = 3

LANE = 128
NEG = -1e30
KEY_TILE = 256
HALO = 16
VMEM_LIMIT = 56 * 1024 * 1024
INT_MIN = -2 ** 31

_NT = (((1,), (1,)), ((), ()))


def _cp(sem, vmem=VMEM_LIMIT):
    return pltpu.CompilerParams(dimension_semantics=sem, vmem_limit_bytes=vmem)


def _rms(x, g):
    return x * lax.rsqrt(jnp.mean(x * x, axis=-1, keepdims=True) + EPS) * g


def _pair_norm(x, g2):
    lo = lax.broadcasted_iota(jnp.int32, x.shape, 1) < HEAD_DIM
    x2 = x * x
    s_lo = jnp.sum(jnp.where(lo, x2, 0.0), axis=-1, keepdims=True)
    s_hi = jnp.sum(jnp.where(lo, 0.0, x2), axis=-1, keepdims=True)
    inv = jnp.where(lo, lax.rsqrt(s_lo / HEAD_DIM + EPS), lax.rsqrt(s_hi / HEAD_DIM + EPS))
    return x * inv * g2


def _norm_matmul_kernel(x_ref, g_ref, w_ref, o_ref, h_ref):
    @pl.when(pl.program_id(1) == 0)
    def _():
        h_ref[...] = _rms(x_ref[...], g_ref[...]).astype(h_ref.dtype)

    o_ref[...] = jnp.dot(h_ref[...], w_ref[...], preferred_element_type=jnp.float32)


def norm_matmul(x, g, w, *, tm, tn):
    m, d = x.shape
    n = w.shape[1]
    return pl.pallas_call(
        _norm_matmul_kernel,
        out_shape=jax.ShapeDtypeStruct((m, n), jnp.float32),
        grid=(m // tm, n // tn),
        in_specs=[
            pl.BlockSpec((tm, d), lambda i, j: (i, 0)),
            pl.BlockSpec((1, d), lambda i, j: (0, 0)),
            pl.BlockSpec((d, tn), lambda i, j: (0, j)),
        ],
        out_specs=pl.BlockSpec((tm, tn), lambda i, j: (i, j)),
        scratch_shapes=[pltpu.VMEM((tm, d), jnp.bfloat16)],
        compiler_params=_cp(("parallel", "arbitrary")),
        name="norm_matmul",
    )(x, g.reshape(1, d), w)


def _matmul_res_kernel(n_pairs, *refs):
    x_ref = refs[2 * n_pairs]
    o_ref = refs[2 * n_pairs + 1]
    acc = x_ref[...]
    for p in range(n_pairs):
        acc = acc + jnp.dot(refs[2 * p][...], refs[2 * p + 1][...], preferred_element_type=jnp.float32)
    o_ref[...] = acc


def matmul_res(pairs, x, *, tm, tn):
    m, n = x.shape
    in_specs, args = [], []
    for a, w in pairs:
        k = a.shape[1]
        in_specs += [pl.BlockSpec((tm, k), lambda i, j: (i, 0)), pl.BlockSpec((k, tn), lambda i, j: (0, j))]
        args += [a, w]
    in_specs.append(pl.BlockSpec((tm, tn), lambda i, j: (i, j)))
    return pl.pallas_call(
        functools.partial(_matmul_res_kernel, len(pairs)),
        out_shape=jax.ShapeDtypeStruct((m, n), jnp.float32),
        grid=(m // tm, n // tn),
        in_specs=in_specs,
        out_specs=pl.BlockSpec((tm, tn), lambda i, j: (i, j)),
        compiler_params=_cp(("parallel", "parallel")),
        name="matmul_res",
    )(*args, x)


def _ffn_kernel(tiles_per_seq, x_ref, xh_ref, g_ref, wg_ref, wu_ref, cg_ref, cu_ref, wd_ref, o_ref, h_ref, acc_ref):
    i = pl.program_id(0)
    f = pl.program_id(1)
    tm = x_ref.shape[0]

    @pl.when(f == 0)
    def _():
        seq_start = (i % tiles_per_seq) == 0
        hh = _rms(xh_ref[...], g_ref[...])
        h_ref[0:HALO, :] = jnp.where(seq_start, 0.0, hh).astype(h_ref.dtype)
        h_ref[HALO:, :] = _rms(x_ref[...], g_ref[...]).astype(h_ref.dtype)
        acc_ref[...] = jnp.zeros_like(acc_ref)

    h = h_ref[...]

    def conv(z, c_ref):
        z1 = pltpu.roll(z, 1, axis=0)
        z2 = pltpu.roll(z, 2, axis=0)
        y = c_ref[2:3, :] * z + c_ref[1:2, :] * z1 + c_ref[0:1, :] * z2
        return y[HALO:, :]

    gate = conv(jnp.dot(h, wg_ref[...], preferred_element_type=jnp.float32), cg_ref)
    up = conv(jnp.dot(h, wu_ref[...], preferred_element_type=jnp.float32), cu_ref)
    a = (gate * jax.nn.sigmoid(gate) * up).astype(jnp.bfloat16)
    acc_ref[...] += jnp.dot(a, wd_ref[...], preferred_element_type=jnp.float32)

    @pl.when(f == pl.num_programs(1) - 1)
    def _():
        o_ref[...] = x_ref[...] + acc_ref[...]


def conv_ffn(x, g, wg, wu, cg, cu, wd, *, seq, tm, tf):
    m, d = x.shape
    fp = wg.shape[1]
    hb = tm // HALO
    return pl.pallas_call(
        functools.partial(_ffn_kernel, seq // tm),
        out_shape=jax.ShapeDtypeStruct((m, d), jnp.float32),
        grid=(m // tm, fp // tf),
        in_specs=[
            pl.BlockSpec((tm, d), lambda i, f: (i, 0)),
            pl.BlockSpec((HALO, d), lambda i, f: (jnp.maximum(i * hb - 1, 0), 0)),
            pl.BlockSpec((1, d), lambda i, f: (0, 0)),
            pl.BlockSpec((d, tf), lambda i, f: (0, f)),
            pl.BlockSpec((d, tf), lambda i, f: (0, f)),
            pl.BlockSpec((CONV_WIDTH, tf), lambda i, f: (0, f)),
            pl.BlockSpec((CONV_WIDTH, tf), lambda i, f: (0, f)),
            pl.BlockSpec((tf, d), lambda i, f: (f, 0)),
        ],
        out_specs=pl.BlockSpec((tm, d), lambda i, f: (i, 0)),
        scratch_shapes=[pltpu.VMEM((tm + HALO, d), jnp.bfloat16), pltpu.VMEM((tm, d), jnp.float32)],
        compiler_params=_cp(("parallel", "arbitrary")),
        name="conv_ffn",
    )(x, x, g.reshape(1, d), wg, wu, cg, cu, wd)


def _ple_kernel(x_ref, g_ref, p_ref, wg_ref, wp_ref, o_ref, h_ref):
    j = pl.program_id(1)
    tn = o_ref.shape[1]

    @pl.when(j == 0)
    def _():
        h_ref[...] = _rms(x_ref[...], g_ref[...]).astype(h_ref.dtype)

    gate = jax.nn.sigmoid(jnp.dot(h_ref[...], wg_ref[...], preferred_element_type=jnp.float32))
    pp = jnp.dot(p_ref[...].astype(jnp.bfloat16), wp_ref[...], preferred_element_type=jnp.float32)
    o_ref[...] = x_ref[:, pl.ds(pl.multiple_of(j * tn, LANE), tn)] + gate * pp


def ple(x, g, p, wg, wp, *, tm, tn):
    m, d = x.shape
    pd = p.shape[1]
    return pl.pallas_call(
        _ple_kernel,
        out_shape=jax.ShapeDtypeStruct((m, d), jnp.float32),
        grid=(m // tm, d // tn),
        in_specs=[
            pl.BlockSpec((tm, d), lambda i, j: (i, 0)),
            pl.BlockSpec((1, d), lambda i, j: (0, 0)),
            pl.BlockSpec((tm, pd), lambda i, j: (i, 0)),
            pl.BlockSpec((d, tn), lambda i, j: (0, j)),
            pl.BlockSpec((pd, tn), lambda i, j: (0, j)),
        ],
        out_specs=pl.BlockSpec((tm, tn), lambda i, j: (i, j)),
        scratch_shapes=[pltpu.VMEM((tm, d), jnp.bfloat16)],
        compiler_params=_cp(("parallel", "arbitrary")),
        name="ple",
    )(x, g.reshape(1, d), p, wg, wp)


def _dsa_prep_kernel(cq_ref, ckv_ref, gcq_ref, gckv_ref, gq_ref, wuq_ref, wqi_ref, q_ref, qi_ref, kv_ref, kvt_ref):
    tm = cq_ref.shape[0]
    cq = _rms(cq_ref[...], gcq_ref[...]).astype(jnp.bfloat16)
    kv = _rms(ckv_ref[...], gckv_ref[...])
    kv_ref[...] = kv.astype(kv_ref.dtype)
    kvt_ref[...] = kv.T.astype(kvt_ref.dtype)
    qi_ref[...] = jnp.dot(cq, wqi_ref[...], preferred_element_type=jnp.float32).astype(qi_ref.dtype)
    att_scale = KV_LORA ** -0.5
    for h in range(H_A):
        ql = jnp.dot(cq, wuq_ref[:, h * KV_LORA:(h + 1) * KV_LORA], preferred_element_type=jnp.float32)
        qn = (_rms(ql, gq_ref[...]) * att_scale).astype(q_ref.dtype)
        for r in range(tm // BLOCK):
            q_ref[(r * H_A + h) * BLOCK:(r * H_A + h + 1) * BLOCK, :] = qn[r * BLOCK:(r + 1) * BLOCK, :]


def dsa_prep(proj, gcq, gckv, gq, wuq, wqi, *, seq, tm):
    m = proj.shape[0]
    cq_blk = 1024 // Q_LORA
    ckv_blk = 1536 // KV_LORA
    nt = seq // tm
    return pl.pallas_call(
        _dsa_prep_kernel,
        out_shape=(
            jax.ShapeDtypeStruct((m * H_A, KV_LORA), jnp.bfloat16),
            jax.ShapeDtypeStruct((m, IDX_HEADS * IDX_DIM), jnp.bfloat16),
            jax.ShapeDtypeStruct((m, KV_LORA), jnp.bfloat16),
            jax.ShapeDtypeStruct((m // seq, KV_LORA, seq), jnp.bfloat16),
        ),
        grid=(m // tm,),
        in_specs=[
            pl.BlockSpec((tm, Q_LORA), lambda i: (i, cq_blk)),
            pl.BlockSpec((tm, KV_LORA), lambda i: (i, ckv_blk)),
            pl.BlockSpec((1, Q_LORA), lambda i: (0, 0)),
            pl.BlockSpec((1, KV_LORA), lambda i: (0, 0)),
            pl.BlockSpec((1, KV_LORA), lambda i: (0, 0)),
            pl.BlockSpec((Q_LORA, H_A * KV_LORA), lambda i: (0, 0)),
            pl.BlockSpec((Q_LORA, IDX_HEADS * IDX_DIM), lambda i: (0, 0)),
        ],
        out_specs=(
            pl.BlockSpec((tm * H_A, KV_LORA), lambda i: (i, 0)),
            pl.BlockSpec((tm, IDX_HEADS * IDX_DIM), lambda i: (i, 0)),
            pl.BlockSpec((tm, KV_LORA), lambda i: (i, 0)),
            pl.BlockSpec((None, KV_LORA, tm), lambda i: (i // nt, 0, i % nt)),
        ),
        compiler_params=_cp(("parallel",)),
        name="dsa_prep",
    )(proj, proj, gcq.reshape(1, -1), gckv.reshape(1, -1), gq.reshape(1, -1), wuq, wqi)


def _dsa_attn_kernel(topk, g_off, q_ref, qi_ref, kwq_ref, kwall_ref, kv_ref, kvt_ref, g_ref, wuv_ref, o_ref,
                     ka_ref, kb_ref, key_ref, m_ref, l_ref, acc_ref):
    n = pl.program_id(1)
    n_tiles = (n + 2) // 2
    q_pos = n * BLOCK + lax.broadcasted_iota(jnp.int32, (KEY_TILE, BLOCK), 1)
    k_row = lax.broadcasted_iota(jnp.int32, (KEY_TILE, BLOCK), 0)

    @pl.when(n == 0)
    def _():
        kw = kwall_ref[...]
        lo = lax.broadcasted_iota(jnp.int32, kw.shape, 1) < IDX_DIM
        ka_ref[...] = jnp.where(lo, kw, 0.0).astype(ka_ref.dtype)
        kb_ref[...] = jnp.where(lo, 0.0, pltpu.roll(kw, IDX_DIM, axis=1)).astype(kb_ref.dtype)

    w_t = kwq_ref[...].T

    def score_tile(j, carry):
        col = pl.multiple_of(j * KEY_TILE, KEY_TILE)
        ka = ka_ref[pl.ds(col, KEY_TILE), :]
        kb = kb_ref[pl.ds(col, KEY_TILE), :]
        s = jnp.zeros((KEY_TILE, BLOCK), jnp.float32)
        for p in range(IDX_HEADS // 2):
            qp = qi_ref[:, p * LANE:(p + 1) * LANE]
            da = lax.dot_general(ka, qp, _NT, preferred_element_type=jnp.float32)
            db = lax.dot_general(kb, qp, _NT, preferred_element_type=jnp.float32)
            s = s + w_t[IDX_DIM + 2 * p:IDX_DIM + 2 * p + 1, :] * jnp.maximum(da, 0.0)
            s = s + w_t[IDX_DIM + 2 * p + 1:IDX_DIM + 2 * p + 2, :] * jnp.maximum(db, 0.0)
        s = jnp.where(col + k_row <= q_pos, s, -jnp.inf)
        bits = pltpu.bitcast(s, jnp.int32)
        key_ref[pl.ds(col, KEY_TILE), :] = jnp.where(bits < 0, bits ^ jnp.int32(0x7FFFFFFF), bits)
        return carry

    lax.fori_loop(0, n_tiles, score_tile, 0)

    def bit_step(b, tu):
        cand_u = tu | lax.shift_left(jnp.int32(1), jnp.int32(31) - b)
        cand_s = cand_u ^ jnp.int32(INT_MIN)

        def count_tile(j, cnt):
            col = pl.multiple_of(j * KEY_TILE, KEY_TILE)
            hit = jnp.where(key_ref[pl.ds(col, KEY_TILE), :] >= cand_s, 1, 0)
            return cnt + jnp.sum(hit.reshape(KEY_TILE // 8, 8, BLOCK), axis=0)

        cnt = lax.fori_loop(0, n_tiles, count_tile, jnp.zeros((8, BLOCK), jnp.int32))
        total = jnp.sum(cnt, axis=0, keepdims=True)
        return jnp.where(total >= topk, cand_u, tu)

    thr = lax.fori_loop(0, 32, bit_step, jnp.zeros((1, BLOCK), jnp.int32)) ^ jnp.int32(INT_MIN)

    m_ref[...] = jnp.full_like(m_ref, NEG)
    l_ref[...] = jnp.zeros_like(l_ref)
    acc_ref[...] = jnp.zeros_like(acc_ref)

    def attn_tile(j, carry):
        col = pl.multiple_of(j * KEY_TILE, KEY_TILE)
        sel = (key_ref[pl.ds(col, KEY_TILE), :] >= thr) & (col + k_row <= q_pos)
        am = jnp.where(sel, 0.0, NEG)
        s = lax.dot_general(kv_ref[pl.ds(col, KEY_TILE), :], q_ref[...], _NT, preferred_element_type=jnp.float32)
        grow = pl.multiple_of(g_off - n * BLOCK + col, LANE)
        s = s + (g_ref[pl.ds(grow, KEY_TILE), :] + jnp.concatenate([am] * H_A, axis=1))
        m_prev = m_ref[...]
        m_new = jnp.maximum(m_prev, jnp.max(s, axis=0, keepdims=True))
        alpha = jnp.exp(m_prev - m_new)
        p = jnp.exp(s - m_new)
        l_ref[...] = alpha * l_ref[...] + jnp.sum(p, axis=0, keepdims=True)
        acc_ref[...] = alpha * acc_ref[...] + jnp.dot(
            kvt_ref[:, pl.ds(col, KEY_TILE)], p.astype(jnp.bfloat16), preferred_element_type=jnp.float32)
        m_ref[...] = m_new
        return carry

    lax.fori_loop(0, n_tiles, attn_tile, 0)

    o_t = (acc_ref[...] / l_ref[...]).astype(jnp.bfloat16)
    for p in range(H_A // 2):
        pair = jnp.concatenate([o_t[:, (2 * p) * BLOCK:(2 * p + 1) * BLOCK],
                                o_t[:, (2 * p + 1) * BLOCK:(2 * p + 2) * BLOCK]], axis=0)
        y_t = jnp.dot(wuv_ref[p], pair, preferred_element_type=jnp.float32)
        o_ref[:, p * LANE:(p + 1) * LANE] = y_t.T.astype(o_ref.dtype)


def dsa_attn(q, qi, proj, kv, kvt, gtab, wuv2, *, batch, seq):
    nb = seq // BLOCK
    topk = min(TOPK_MAX, seq // TOPK_DIV)
    kw_blk = 1792 // LANE
    g_off = seq - BLOCK
    return pl.pallas_call(
        functools.partial(_dsa_attn_kernel, topk, g_off),
        out_shape=jax.ShapeDtypeStruct((batch * seq, H_A * HEAD_DIM), jnp.bfloat16),
        grid=(batch, nb),
        in_specs=[
            pl.BlockSpec((H_A * BLOCK, KV_LORA), lambda b, n: (b * nb + n, 0)),
            pl.BlockSpec((BLOCK, IDX_HEADS * IDX_DIM), lambda b, n: (b * nb + n, 0)),
            pl.BlockSpec((BLOCK, LANE), lambda b, n: (b * nb + n, kw_blk)),
            pl.BlockSpec((seq, LANE), lambda b, n: (b, kw_blk)),
            pl.BlockSpec((seq, KV_LORA), lambda b, n: (b, 0)),
            pl.BlockSpec((None, KV_LORA, seq), lambda b, n: (b, 0, 0)),
            pl.BlockSpec(gtab.shape, lambda b, n: (0, 0), pipeline_mode=pl.Buffered(1)),
            pl.BlockSpec(wuv2.shape, lambda b, n: (0, 0, 0)),
        ],
        out_specs=pl.BlockSpec((BLOCK, H_A * HEAD_DIM), lambda b, n: (b * nb + n, 0)),
        scratch_shapes=[
            pltpu.VMEM((seq, LANE), jnp.bfloat16),
            pltpu.VMEM((seq, LANE), jnp.bfloat16),
            pltpu.VMEM((seq, BLOCK), jnp.int32),
            pltpu.VMEM((1, H_A * BLOCK), jnp.float32),
            pltpu.VMEM((1, H_A * BLOCK), jnp.float32),
            pltpu.VMEM((KV_LORA, H_A * BLOCK), jnp.float32),
        ],
        compiler_params=_cp(("parallel", "arbitrary")),
        name="dsa_attn",
    )(q, qi, proj, proj, kv, kvt, gtab, wuv2)


def _swa_kernel(q_ref, kc_ref, kp_ref, vc_ref, vp_ref, gq_ref, gk_ref, bias_ref, sink_ref, o_ref):
    n = pl.program_id(1)
    g = H_B // KV_B
    scale = HEAD_DIM ** -0.5
    kn = _pair_norm(jnp.concatenate([kp_ref[...], kc_ref[...]], axis=0), gk_ref[...])
    k_nat = kn.astype(jnp.bfloat16)
    k_swap = pltpu.roll(kn, HEAD_DIM, axis=1).astype(jnp.bfloat16)
    v = jnp.concatenate([vp_ref[...], vc_ref[...]], axis=0).astype(jnp.bfloat16)
    col = lax.broadcasted_iota(jnp.int32, (BLOCK, 2 * BLOCK), 1)
    first = jnp.where((n == 0) & (col < BLOCK), NEG, 0.0)
    lo = lax.broadcasted_iota(jnp.int32, (BLOCK, LANE), 1) < HEAD_DIM
    for p in range(H_B // 2):
        hk = (2 * p) // g
        qn = _pair_norm(q_ref[:, p * LANE:(p + 1) * LANE], gq_ref[...]) * scale
        halves = []
        for half in range(2):
            head = 2 * p + half
            qh = (jnp.where(lo, qn, 0.0) if half == 0 else jnp.where(lo, 0.0, qn)).astype(jnp.bfloat16)
            kh = k_nat if half == hk else k_swap
            s = lax.dot_general(qh, kh, _NT, preferred_element_type=jnp.float32) + bias_ref[head] + first
            sink = sink_ref[head]
            mx = jnp.maximum(jnp.max(s, axis=-1, keepdims=True), sink)
            e = jnp.exp(s - mx)
            den = jnp.sum(e, axis=-1, keepdims=True) + jnp.exp(sink - mx)
            o = jnp.dot((e / den).astype(jnp.bfloat16), v, preferred_element_type=jnp.float32)
            halves.append(o if half == hk else pltpu.roll(o, HEAD_DIM, axis=1))
        o_ref[:, p * LANE:(p + 1) * LANE] = jnp.where(lo, halves[0], halves[1]).astype(o_ref.dtype)


def swa_attn(proj, gq, gk, bias_tab, sinks, *, batch, seq):
    nb = seq // BLOCK
    k_blk = 1920 // LANE
    v_blk = 2048 // LANE
    prev = lambda b, n: b * nb + jnp.maximum(n - 1, 0)
    return pl.pallas_call(
        _swa_kernel,
        out_shape=jax.ShapeDtypeStruct((batch * seq, H_B * HEAD_DIM), jnp.bfloat16),
        grid=(batch, nb),
        in_specs=[
            pl.BlockSpec((BLOCK, H_B * HEAD_DIM), lambda b, n: (b * nb + n, 0)),
            pl.BlockSpec((BLOCK, LANE), lambda b, n: (b * nb + n, k_blk)),
            pl.BlockSpec((BLOCK, LANE), lambda b, n: (prev(b, n), k_blk)),
            pl.BlockSpec((BLOCK, LANE), lambda b, n: (b * nb + n, v_blk)),
            pl.BlockSpec((BLOCK, LANE), lambda b, n: (prev(b, n), v_blk)),
            pl.BlockSpec((1, LANE), lambda b, n: (0, 0)),
            pl.BlockSpec((1, LANE), lambda b, n: (0, 0)),
            pl.BlockSpec(bias_tab.shape, lambda b, n: (0, 0, 0)),
            pl.BlockSpec(memory_space=pltpu.SMEM),
        ],
        out_specs=pl.BlockSpec((BLOCK, H_B * HEAD_DIM), lambda b, n: (b * nb + n, 0)),
        compiler_params=_cp(("parallel", "parallel")),
        name="swa_attn",
    )(proj, proj, proj, proj, proj, jnp.tile(gq, 2).reshape(1, LANE), jnp.tile(gk, 2).reshape(1, LANE),
      bias_tab, sinks)


def _split3(c):
    hi = c.astype(jnp.bfloat16).astype(jnp.float32)
    r = c - hi
    mid = r.astype(jnp.bfloat16).astype(jnp.float32)
    return hi, mid, r - mid


def _fox_prep_kernel(q_ref, k_ref, v_ref, f_ref, gq_ref, gk_ref, fb_ref, qo_ref, ko_ref, vt_ref, carry_ref):
    tm = q_ref.shape[0]

    @pl.when(pl.program_id(1) == 0)
    def _():
        carry_ref[...] = jnp.zeros_like(carry_ref)

    z = f_ref[...] + fb_ref[...]
    logf = jnp.minimum(z, 0.0) - jnp.log1p(jnp.exp(-jnp.abs(z)))
    r = lax.broadcasted_iota(jnp.int32, (tm, tm), 0)
    c = lax.broadcasted_iota(jnp.int32, (tm, tm), 1)
    tri = jnp.where(c <= r, 1.0, 0.0)
    cum = jnp.dot(tri, logf, preferred_element_type=jnp.float32, precision=lax.Precision.HIGHEST) + carry_ref[...]
    carry_ref[...] = cum[tm - 1:tm, :]

    scale = HEAD_DIM ** -0.5
    lane = lax.broadcasted_iota(jnp.int32, (tm, LANE), 1)
    for p in range(H_C // 2):
        sl = slice(p * LANE, (p + 1) * LANE)
        qn = _pair_norm(q_ref[:, sl], gq_ref[...]) * scale
        kn = _pair_norm(k_ref[:, sl], gk_ref[...])
        for half in range(2):
            h = 2 * p + half
            qh = qn if half == 0 else pltpu.roll(qn, HEAD_DIM, axis=1)
            kh = kn if half == 0 else pltpu.roll(kn, HEAD_DIM, axis=1)
            hi, mid, lo = _split3(cum[:, h:h + 1])
            d = HEAD_DIM
            qa = jnp.where(lane < d, qh, jnp.where(lane == d, hi, jnp.where(lane == d + 1, mid, jnp.where(
                lane == d + 2, lo, jnp.where(lane < d + 6, 1.0, 0.0)))))
            ka = jnp.where(lane < d, kh, jnp.where(lane < d + 3, 1.0, jnp.where(lane == d + 3, -hi, jnp.where(
                lane == d + 4, -mid, jnp.where(lane == d + 5, -lo, 0.0)))))
            qo_ref[:, h * LANE:(h + 1) * LANE] = qa.astype(qo_ref.dtype)
            ko_ref[:, h * LANE:(h + 1) * LANE] = ka.astype(ko_ref.dtype)
    vt_ref[...] = v_ref[...].T.astype(vt_ref.dtype)


def fox_prep(proj, gq, gk, fbias, *, batch, seq, tm):
    m = proj.shape[0]
    hd = H_C * HEAD_DIM
    nt = seq // tm
    f_blk = 3 * hd // LANE
    fb = jnp.zeros((1, LANE), jnp.float32).at[0, :H_C].set(fbias)
    aug = jax.ShapeDtypeStruct((m, H_C * LANE), jnp.bfloat16)
    row = lambda b, i: b * nt + i
    return pl.pallas_call(
        _fox_prep_kernel,
        out_shape=(aug, aug, jax.ShapeDtypeStruct((batch, hd, seq), jnp.bfloat16)),
        grid=(batch, nt),
        in_specs=[
            pl.BlockSpec((tm, hd), lambda b, i: (row(b, i), 0)),
            pl.BlockSpec((tm, hd), lambda b, i: (row(b, i), 1)),
            pl.BlockSpec((tm, hd), lambda b, i: (row(b, i), 2)),
            pl.BlockSpec((tm, LANE), lambda b, i: (row(b, i), f_blk)),
            pl.BlockSpec((1, LANE), lambda b, i: (0, 0)),
            pl.BlockSpec((1, LANE), lambda b, i: (0, 0)),
            pl.BlockSpec((1, LANE), lambda b, i: (0, 0)),
        ],
        out_specs=(
            pl.BlockSpec((tm, H_C * LANE), lambda b, i: (row(b, i), 0)),
            pl.BlockSpec((tm, H_C * LANE), lambda b, i: (row(b, i), 0)),
            pl.BlockSpec((None, hd, tm), lambda b, i: (b, 0, i)),
        ),
        scratch_shapes=[pltpu.VMEM((1, LANE), jnp.float32)],
        compiler_params=_cp(("parallel", "arbitrary")),
        name="fox_prep",
    )(proj, proj, proj, proj, jnp.tile(gq, 2).reshape(1, LANE), jnp.tile(gk, 2).reshape(1, LANE), fb)


def _fox_attn_kernel(tq, q_ref, k_ref, vt_ref, o_ref):
    seq = q_ref.shape[0]
    k_row = lax.broadcasted_iota(jnp.int32, (tq, tq), 0)
    q_col = lax.broadcasted_iota(jnp.int32, (tq, tq), 1)

    def q_tile(i, carry):
        row = pl.multiple_of(i * tq, tq)
        qs = [q_ref[pl.ds(row, tq), hh * LANE:(hh + 1) * LANE] for hh in range(2)]

        def tile(j, state, masked):
            col = pl.multiple_of(j * tq, tq)
            new = []
            for hh in range(2):
                m_prev, l_prev, acc = state[hh]
                s = lax.dot_general(k_ref[pl.ds(col, tq), hh * LANE:(hh + 1) * LANE], qs[hh], _NT,
                                    preferred_element_type=jnp.float32)
                if masked:
                    s = jnp.where(k_row <= q_col, s, NEG)
                m_new = jnp.maximum(m_prev, jnp.max(s, axis=0, keepdims=True))
                alpha = jnp.exp(m_prev - m_new)
                p = jnp.exp(s - m_new)
                l_new = alpha * l_prev + jnp.sum(p, axis=0, keepdims=True)
                vt = vt_ref[hh * HEAD_DIM:(hh + 1) * HEAD_DIM, pl.ds(col, tq)]
                acc = alpha * acc + jnp.dot(vt, p.astype(jnp.bfloat16), preferred_element_type=jnp.float32)
                new.append((m_new, l_new, acc))
            return tuple(new)

        init = tuple((jnp.full((1, tq), NEG, jnp.float32), jnp.zeros((1, tq), jnp.float32),
                      jnp.zeros((HEAD_DIM, tq), jnp.float32)) for _ in range(2))
        state = lax.fori_loop(0, i, lambda j, st: tile(j, st, False), init)
        state = tile(i, state, True)
        out_t = jnp.concatenate([state[0][2] / state[0][1], state[1][2] / state[1][1]], axis=0)
        o_ref[pl.ds(row, tq), :] = out_t.T.astype(o_ref.dtype)
        return carry

    lax.fori_loop(0, seq // tq, q_tile, 0)


def fox_attn(qa, ka, vt, *, batch, seq, tq):
    pairs = H_C // 2
    return pl.pallas_call(
        functools.partial(_fox_attn_kernel, tq),
        out_shape=jax.ShapeDtypeStruct((batch * seq, H_C * HEAD_DIM), jnp.bfloat16),
        grid=(batch, pairs),
        in_specs=[
            pl.BlockSpec((seq, 2 * LANE), lambda b, c: (b, c)),
            pl.BlockSpec((seq, 2 * LANE), lambda b, c: (b, c)),
            pl.BlockSpec((None, 2 * HEAD_DIM, seq), lambda b, c: (b, c, 0)),
        ],
        out_specs=pl.BlockSpec((seq, 2 * HEAD_DIM), lambda b, c: (b, c)),
        compiler_params=_cp(("parallel", "parallel")),
        name="fox_attn",
    )(qa, ka, vt)


def _rel_bucket(dist):
    n = jnp.maximum(dist, 0)
    exact = N_BUCKETS // 2
    nf = jnp.maximum(n, 1).astype(jnp.float32)
    large = exact + (jnp.log(nf / exact) / math.log(MAX_DISTANCE / exact) * (N_BUCKETS - exact)).astype(jnp.int32)
    large = jnp.minimum(large, N_BUCKETS - 1)
    return jnp.where(n < exact, n, large)


def _pad_cols(w, n):
    return jnp.pad(w, ((0, 0), (0, n - w.shape[1])))


def _toeplitz(v, rows, cols, off):
    h, dmax = v.shape
    lt = cols + rows - 1
    idx = off + rows - 1 - np.arange(lt)
    e = jnp.where(((idx >= 0) & (idx < dmax))[None], v[:, np.clip(idx, 0, dmax - 1)], 0.0)
    f = jnp.pad(e, ((0, 0), (0, 1)))
    a = jnp.tile(f, (1, rows))[:, :rows * lt].reshape(h, rows, lt)
    return a[:, :, rows - 1:rows - 1 + cols]


def kernel(x, p, attn_norm, ffn_norm, ple_norm, rel_bias, w_in_even, a_cq_norm, a_ckv_norm, a_w_uq, a_q_norm,
           a_w_qidx, a_w_uv, b_q_norm, b_k_norm, b_sinks, w_out_even, w_in_odd, c_forget_bias, c_q_norm,
           c_k_norm, w_out_odd, w_up, ffn_conv, w_down, w_ple_gate, w_ple_proj):
    batch, seq, d = x.shape
    depth = p.shape[0]
    m = batch * seq
    bf = jnp.bfloat16
    d_ff = w_down.shape[1]
    f_pad = -(-d_ff // 512) * 512
    hd_c = H_C * HEAD_DIM

    bias_by_dist = rel_bias[_rel_bucket(jnp.arange(seq))].T
    g_off = seq - BLOCK
    g_cols = g_off + KEY_TILE
    gtab = jnp.transpose(_toeplitz(bias_by_dist[:H_A], BLOCK, g_cols, g_off), (2, 0, 1)).reshape(g_cols, H_A * BLOCK)
    dist_b = np.arange(BLOCK)[:, None] + BLOCK - np.arange(2 * BLOCK)[None, :]
    band = (dist_b >= 0) & (dist_b < WINDOW)
    btab = jnp.where(band[None], _toeplitz(bias_by_dist[H_A:, :WINDOW], BLOCK, 2 * BLOCK, BLOCK), NEG)

    o1 = Q_LORA
    o2 = o1 + KV_LORA
    o3 = o2 + IDX_DIM
    o4 = o3 + IDX_HEADS
    o5 = o4 + H_B * HEAD_DIM
    o6 = o5 + KV_B * HEAD_DIM

    x = x.reshape(m, d)
    for i in range(depth):
        if i % 2 == 0:
            e = i // 2
            w = w_in_even[e]
            w_in = jnp.concatenate(
                [w[:, o4:o5], w[:, :o1], w[:, o1:o2], _pad_cols(w[:, o2:o4], LANE), w[:, o5:o6], w[:, o6:],
                 jnp.zeros((d, LANE), w.dtype)], axis=1).astype(bf)
            proj = norm_matmul(x, attn_norm[i], w_in, tm=1024, tn=w_in.shape[1] // 3)
            q, qi, kv, kvt = dsa_prep(proj, a_cq_norm[e], a_ckv_norm[e], a_q_norm[e], a_w_uq[e].astype(bf),
                                      a_w_qidx[e].astype(bf), seq=seq, tm=256)
            wuv = jnp.swapaxes(a_w_uv[e], 1, 2).reshape(H_A // 2, 2, HEAD_DIM, KV_LORA)
            zero = jnp.zeros_like(wuv[:, 0])
            wuv2 = jnp.concatenate([jnp.concatenate([wuv[:, 0], zero], axis=2),
                                    jnp.concatenate([zero, wuv[:, 1]], axis=2)], axis=1).astype(bf)
            y_a = dsa_attn(q, qi, proj, kv, kvt, gtab, wuv2, batch=batch, seq=seq)
            y_b = swa_attn(proj, b_q_norm[e], b_k_norm[e], btab, b_sinks[e], batch=batch, seq=seq)
            wo = w_out_even[e].astype(bf)
            x = matmul_res([(y_a, wo[:H_A * HEAD_DIM]), (y_b, wo[H_A * HEAD_DIM:])], x, tm=1024, tn=512)
        else:
            o = i // 2
            w_in = jnp.pad(w_in_odd[o], ((0, 0), (0, 3 * hd_c + LANE - w_in_odd[o].shape[1]))).astype(bf)
            proj = norm_matmul(x, attn_norm[i], w_in, tm=1024, tn=w_in.shape[1] // 7)
            qa, ka, vt = fox_prep(proj, c_q_norm[o], c_k_norm[o], c_forget_bias[o], batch=batch, seq=seq, tm=256)
            y = fox_attn(qa, ka, vt, batch=batch, seq=seq, tq=256)
            x = matmul_res([(y, w_out_odd[o].astype(bf))], x, tm=1024, tn=512)

        wg = _pad_cols(w_up[i][:, :d_ff], f_pad).astype(bf)
        wu = _pad_cols(w_up[i][:, d_ff:], f_pad).astype(bf)
        cg = _pad_cols(ffn_conv[i][:, :d_ff], f_pad)
        cu = _pad_cols(ffn_conv[i][:, d_ff:], f_pad)
        wd = jnp.pad(w_down[i], ((0, f_pad - d_ff), (0, 0))).astype(bf)
        x = conv_ffn(x, ffn_norm[i], wg, wu, cg, cu, wd, seq=seq, tm=512, tf=512)
        x = ple(x, ple_norm[i], p[i].reshape(m, -1), w_ple_gate[i].astype(bf), w_ple_proj[i].astype(bf),
                tm=1024, tn=512)
    return x.reshape(batch, seq, d)
```

```python
import functools
import math

import jax
import jax.numpy as jnp
import numpy as np
from jax import lax
from jax.experimental import pallas as pl
from jax.experimental.pallas import tpu as pltpu

HEAD_DIM = 64
BLOCK = 128
EPS = 1e-6
H_A = 16
Q_LORA = 512
KV_LORA = 256
IDX_HEADS = 16
IDX_DIM = 64
TOPK_MAX = 256
TOPK_DIV = 4
H_B = 16
KV_B = 2
WINDOW = 128
H_C = 32
N_BUCKETS = 32
MAX_DISTANCE = 1024
CONV_WIDTH = 3

LANE = 128
NEG = -1e30
KEY_TILE = 256
HALO = 16
V_ROWS = HEAD_DIM + 16
VMEM_LIMIT = 56 * 1024 * 1024
INT_MIN = -2 ** 31

_NT = (((1,), (1,)), ((), ()))


def _cp(sem, vmem=VMEM_LIMIT):
    return pltpu.CompilerParams(dimension_semantics=sem, vmem_limit_bytes=vmem)


def _rms(x, g):
    return x * lax.rsqrt(jnp.mean(x * x, axis=-1, keepdims=True) + EPS) * g


def _pair_norm(x, g2):
    lo = lax.broadcasted_iota(jnp.int32, x.shape, 1) < HEAD_DIM
    x2 = x * x
    s_lo = jnp.sum(jnp.where(lo, x2, 0.0), axis=-1, keepdims=True)
    s_hi = jnp.sum(jnp.where(lo, 0.0, x2), axis=-1, keepdims=True)
    inv = jnp.where(lo, lax.rsqrt(s_lo / HEAD_DIM + EPS), lax.rsqrt(s_hi / HEAD_DIM + EPS))
    return x * inv * g2


def _norm_matmul_kernel(x_ref, g_ref, w_ref, o_ref, h_ref):
    @pl.when(pl.program_id(1) == 0)
    def _():
        h_ref[...] = _rms(x_ref[...], g_ref[...]).astype(h_ref.dtype)

    o_ref[...] = jnp.dot(h_ref[...], w_ref[...], preferred_element_type=jnp.float32)


def norm_matmul(x, g, w, *, tm, tn):
    m, d = x.shape
    n = w.shape[1]
    return pl.pallas_call(
        _norm_matmul_kernel,
        out_shape=jax.ShapeDtypeStruct((m, n), jnp.float32),
        grid=(m // tm, n // tn),
        in_specs=[
            pl.BlockSpec((tm, d), lambda i, j: (i, 0)),
            pl.BlockSpec((1, d), lambda i, j: (0, 0)),
            pl.BlockSpec((d, tn), lambda i, j: (0, j)),
        ],
        out_specs=pl.BlockSpec((tm, tn), lambda i, j: (i, j)),
        scratch_shapes=[pltpu.VMEM((tm, d), jnp.bfloat16)],
        compiler_params=_cp(("parallel", "arbitrary")),
        name="norm_matmul",
    )(x, g.reshape(1, d), w)


def _matmul_res_kernel(n_pairs, *refs):
    x_ref = refs[2 * n_pairs]
    o_ref = refs[2 * n_pairs + 1]
    acc = x_ref[...]
    for p in range(n_pairs):
        acc = acc + jnp.dot(refs[2 * p][...], refs[2 * p + 1][...], preferred_element_type=jnp.float32)
    o_ref[...] = acc


def matmul_res(pairs, x, *, tm, tn):
    m, n = x.shape
    in_specs, args = [], []
    for a, w in pairs:
        k = a.shape[1]
        in_specs += [pl.BlockSpec((tm, k), lambda i, j: (i, 0)), pl.BlockSpec((k, tn), lambda i, j: (0, j))]
        args += [a, w]
    in_specs.append(pl.BlockSpec((tm, tn), lambda i, j: (i, j)))
    return pl.pallas_call(
        functools.partial(_matmul_res_kernel, len(pairs)),
        out_shape=jax.ShapeDtypeStruct((m, n), jnp.float32),
        grid=(m // tm, n // tn),
        in_specs=in_specs,
        out_specs=pl.BlockSpec((tm, tn), lambda i, j: (i, j)),
        compiler_params=_cp(("parallel", "parallel")),
        name="matmul_res",
    )(*args, x)


def _ffn_kernel(tiles_per_seq, x_ref, xh_ref, g_ref, wg_ref, wu_ref, cg_ref, cu_ref, wd_ref, o_ref, h_ref, acc_ref):
    i = pl.program_id(0)
    f = pl.program_id(1)
    tm = x_ref.shape[0]

    @pl.when(f == 0)
    def _():
        seq_start = (i % tiles_per_seq) == 0
        hh = _rms(xh_ref[...], g_ref[...])
        h_ref[0:HALO, :] = jnp.where(seq_start, 0.0, hh).astype(h_ref.dtype)
        h_ref[HALO:, :] = _rms(x_ref[...], g_ref[...]).astype(h_ref.dtype)
        acc_ref[...] = jnp.zeros_like(acc_ref)

    h = h_ref[...]

    def conv(z, c_ref):
        z1 = pltpu.roll(z, 1, axis=0)
        z2 = pltpu.roll(z, 2, axis=0)
        y = c_ref[2:3, :] * z + c_ref[1:2, :] * z1 + c_ref[0:1, :] * z2
        return y[HALO:, :]

    gate = conv(jnp.dot(h, wg_ref[...], preferred_element_type=jnp.float32), cg_ref)
    up = conv(jnp.dot(h, wu_ref[...], preferred_element_type=jnp.float32), cu_ref)
    a = (gate * jax.nn.sigmoid(gate) * up).astype(jnp.bfloat16)
    acc_ref[...] += jnp.dot(a, wd_ref[...], preferred_element_type=jnp.float32)

    @pl.when(f == pl.num_programs(1) - 1)
    def _():
        o_ref[...] = x_ref[...] + acc_ref[...]


def conv_ffn(x, g, wg, wu, cg, cu, wd, *, seq, tm, tf):
    m, d = x.shape
    fp = wg.shape[1]
    hb = tm // HALO
    return pl.pallas_call(
        functools.partial(_ffn_kernel, seq // tm),
        out_shape=jax.ShapeDtypeStruct((m, d), jnp.float32),
        grid=(m // tm, fp // tf),
        in_specs=[
            pl.BlockSpec((tm, d), lambda i, f: (i, 0)),
            pl.BlockSpec((HALO, d), lambda i, f: (jnp.maximum(i * hb - 1, 0), 0)),
            pl.BlockSpec((1, d), lambda i, f: (0, 0)),
            pl.BlockSpec((d, tf), lambda i, f: (0, f)),
            pl.BlockSpec((d, tf), lambda i, f: (0, f)),
            pl.BlockSpec((CONV_WIDTH, tf), lambda i, f: (0, f)),
            pl.BlockSpec((CONV_WIDTH, tf), lambda i, f: (0, f)),
            pl.BlockSpec((tf, d), lambda i, f: (f, 0)),
        ],
        out_specs=pl.BlockSpec((tm, d), lambda i, f: (i, 0)),
        scratch_shapes=[pltpu.VMEM((tm + HALO, d), jnp.bfloat16), pltpu.VMEM((tm, d), jnp.float32)],
        compiler_params=_cp(("parallel", "arbitrary")),
        name="conv_ffn",
    )(x, x, g.reshape(1, d), wg, wu, cg, cu, wd)


def _ple_kernel(x_ref, g_ref, p_ref, wg_ref, wp_ref, o_ref, h_ref):
    j = pl.program_id(1)
    tn = o_ref.shape[1]

    @pl.when(j == 0)
    def _():
        h_ref[...] = _rms(x_ref[...], g_ref[...]).astype(h_ref.dtype)

    gate = jax.nn.sigmoid(jnp.dot(h_ref[...], wg_ref[...], preferred_element_type=jnp.float32))
    pp = jnp.dot(p_ref[...].astype(jnp.bfloat16), wp_ref[...], preferred_element_type=jnp.float32)
    o_ref[...] = x_ref[:, pl.ds(pl.multiple_of(j * tn, LANE), tn)] + gate * pp


def ple(x, g, p, wg, wp, *, tm, tn):
    m, d = x.shape
    pd = p.shape[1]
    return pl.pallas_call(
        _ple_kernel,
        out_shape=jax.ShapeDtypeStruct((m, d), jnp.float32),
        grid=(m // tm, d // tn),
        in_specs=[
            pl.BlockSpec((tm, d), lambda i, j: (i, 0)),
            pl.BlockSpec((1, d), lambda i, j: (0, 0)),
            pl.BlockSpec((tm, pd), lambda i, j: (i, 0)),
            pl.BlockSpec((d, tn), lambda i, j: (0, j)),
            pl.BlockSpec((pd, tn), lambda i, j: (0, j)),
        ],
        out_specs=pl.BlockSpec((tm, tn), lambda i, j: (i, j)),
        scratch_shapes=[pltpu.VMEM((tm, d), jnp.bfloat16)],
        compiler_params=_cp(("parallel", "arbitrary")),
        name="ple",
    )(x, g.reshape(1, d), p, wg, wp)


def _dsa_prep_kernel(cq_ref, ckv_ref, gcq_ref, gckv_ref, gq_ref, wuq_ref, wqi_ref, q_ref, qi_ref, kv_ref, kvt_ref):
    tm = cq_ref.shape[0]
    cq = _rms(cq_ref[...], gcq_ref[...]).astype(jnp.bfloat16)
    kv = _rms(ckv_ref[...], gckv_ref[...])
    kv_ref[...] = kv.astype(kv_ref.dtype)
    kvt_ref[...] = kv.T.astype(kvt_ref.dtype)
    qi_ref[...] = jnp.dot(cq, wqi_ref[...], preferred_element_type=jnp.float32).astype(qi_ref.dtype)
    att_scale = KV_LORA ** -0.5
    for h in range(H_A):
        ql = jnp.dot(cq, wuq_ref[:, h * KV_LORA:(h + 1) * KV_LORA], preferred_element_type=jnp.float32)
        qn = (_rms(ql, gq_ref[...]) * att_scale).astype(q_ref.dtype)
        for r in range(tm // BLOCK):
            q_ref[(r * H_A + h) * BLOCK:(r * H_A + h + 1) * BLOCK, :] = qn[r * BLOCK:(r + 1) * BLOCK, :]


def dsa_prep(proj, gcq, gckv, gq, wuq, wqi, *, seq, tm):
    m = proj.shape[0]
    cq_blk = 1024 // Q_LORA
    ckv_blk = 1536 // KV_LORA
    nt = seq // tm
    return pl.pallas_call(
        _dsa_prep_kernel,
        out_shape=(
            jax.ShapeDtypeStruct((m * H_A, KV_LORA), jnp.bfloat16),
            jax.ShapeDtypeStruct((m, IDX_HEADS * IDX_DIM), jnp.bfloat16),
            jax.ShapeDtypeStruct((m, KV_LORA), jnp.bfloat16),
            jax.ShapeDtypeStruct((m // seq, KV_LORA, seq), jnp.bfloat16),
        ),
        grid=(m // tm,),
        in_specs=[
            pl.BlockSpec((tm, Q_LORA), lambda i: (i, cq_blk)),
            pl.BlockSpec((tm, KV_LORA), lambda i: (i, ckv_blk)),
            pl.BlockSpec((1, Q_LORA), lambda i: (0, 0)),
            pl.BlockSpec((1, KV_LORA), lambda i: (0, 0)),
            pl.BlockSpec((1, KV_LORA), lambda i: (0, 0)),
            pl.BlockSpec((Q_LORA, H_A * KV_LORA), lambda i: (0, 0)),
            pl.BlockSpec((Q_LORA, IDX_HEADS * IDX_DIM), lambda i: (0, 0)),
        ],
        out_specs=(
            pl.BlockSpec((tm * H_A, KV_LORA), lambda i: (i, 0)),
            pl.BlockSpec((tm, IDX_HEADS * IDX_DIM), lambda i: (i, 0)),
            pl.BlockSpec((tm, KV_LORA), lambda i: (i, 0)),
            pl.BlockSpec((None, KV_LORA, tm), lambda i: (i // nt, 0, i % nt)),
        ),
        compiler_params=_cp(("parallel",)),
        name="dsa_prep",
    )(proj, proj, gcq.reshape(1, -1), gckv.reshape(1, -1), gq.reshape(1, -1), wuq, wqi)


def _dsa_attn_kernel(topk, g_off, q_ref, qi_ref, kwq_ref, kwall_ref, kv_ref, kvt_ref, g_ref, wuv_ref, o_ref,
                     ka_ref, kb_ref, key_ref, m_ref, l_ref, acc_ref):
    n = pl.program_id(1)
    n_tiles = (n + 2) // 2
    q_pos = n * BLOCK + lax.broadcasted_iota(jnp.int32, (KEY_TILE, BLOCK), 1)
    k_row = lax.broadcasted_iota(jnp.int32, (KEY_TILE, BLOCK), 0)

    @pl.when(n == 0)
    def _():
        kw = kwall_ref[...]
        lo = lax.broadcasted_iota(jnp.int32, kw.shape, 1) < IDX_DIM
        ka_ref[...] = jnp.where(lo, kw, 0.0).astype(ka_ref.dtype)
        kb_ref[...] = jnp.where(lo, 0.0, pltpu.roll(kw, IDX_DIM, axis=1)).astype(kb_ref.dtype)

    w_t = kwq_ref[...].T

    def score_tile(j, carry):
        col = pl.multiple_of(j * KEY_TILE, KEY_TILE)
        ka = ka_ref[pl.ds(col, KEY_TILE), :]
        kb = kb_ref[pl.ds(col, KEY_TILE), :]
        s = jnp.zeros((KEY_TILE, BLOCK), jnp.float32)
        for p in range(IDX_HEADS // 2):
            qp = qi_ref[:, p * LANE:(p + 1) * LANE]
            da = lax.dot_general(ka, qp, _NT, preferred_element_type=jnp.float32)
            db = lax.dot_general(kb, qp, _NT, preferred_element_type=jnp.float32)
            s = s + w_t[IDX_DIM + 2 * p:IDX_DIM + 2 * p + 1, :] * jnp.maximum(da, 0.0)
            s = s + w_t[IDX_DIM + 2 * p + 1:IDX_DIM + 2 * p + 2, :] * jnp.maximum(db, 0.0)
        s = jnp.where(col + k_row <= q_pos, s, -jnp.inf)
        bits = pltpu.bitcast(s, jnp.int32)
        key_ref[pl.ds(col, KEY_TILE), :] = jnp.where(bits < 0, bits ^ jnp.int32(0x7FFFFFFF), bits)
        return carry

    lax.fori_loop(0, n_tiles, score_tile, 0)

    def bit_step(b, tu):
        cand_u = tu | lax.shift_left(jnp.int32(1), jnp.int32(31) - b)
        cand_s = cand_u ^ jnp.int32(INT_MIN)

        def count_tile(j, cnt):
            col = pl.multiple_of(j * KEY_TILE, KEY_TILE)
            hit = jnp.where(key_ref[pl.ds(col, KEY_TILE), :] >= cand_s, 1, 0)
            return cnt + jnp.sum(hit.reshape(KEY_TILE // 8, 8, BLOCK), axis=0)

        cnt = lax.fori_loop(0, n_tiles, count_tile, jnp.zeros((8, BLOCK), jnp.int32))
        total = jnp.sum(cnt, axis=0, keepdims=True)
        return jnp.where(total >= topk, cand_u, tu)

    thr = lax.fori_loop(0, 32, bit_step, jnp.zeros((1, BLOCK), jnp.int32)) ^ jnp.int32(INT_MIN)

    m_ref[...] = jnp.full_like(m_ref, NEG)
    l_ref[...] = jnp.zeros_like(l_ref)
    acc_ref[...] = jnp.zeros_like(acc_ref)

    def attn_tile(j, carry):
        col = pl.multiple_of(j * KEY_TILE, KEY_TILE)
        sel = (key_ref[pl.ds(col, KEY_TILE), :] >= thr) & (col + k_row <= q_pos)
        am = jnp.where(sel, 0.0, NEG)
        s = lax.dot_general(kv_ref[pl.ds(col, KEY_TILE), :], q_ref[...], _NT, preferred_element_type=jnp.float32)
        grow = pl.multiple_of(g_off - n * BLOCK + col, LANE)
        s = s + (g_ref[pl.ds(grow, KEY_TILE), :] + jnp.concatenate([am] * H_A, axis=1))
        m_prev = m_ref[...]
        m_new = jnp.maximum(m_prev, jnp.max(s, axis=0, keepdims=True))
        alpha = jnp.exp(m_prev - m_new)
        p = jnp.exp(s - m_new)
        l_ref[...] = alpha * l_ref[...] + jnp.sum(p, axis=0, keepdims=True)
        acc_ref[...] = alpha * acc_ref[...] + jnp.dot(
            kvt_ref[:, pl.ds(col, KEY_TILE)], p.astype(jnp.bfloat16), preferred_element_type=jnp.float32)
        m_ref[...] = m_new
        return carry

    lax.fori_loop(0, n_tiles, attn_tile, 0)

    o_t = (acc_ref[...] / l_ref[...]).astype(jnp.bfloat16)
    for p in range(H_A // 2):
        pair = jnp.concatenate([o_t[:, (2 * p) * BLOCK:(2 * p + 1) * BLOCK],
                                o_t[:, (2 * p + 1) * BLOCK:(2 * p + 2) * BLOCK]], axis=0)
        y_t = jnp.dot(wuv_ref[p], pair, preferred_element_type=jnp.float32)
        o_ref[:, p * LANE:(p + 1) * LANE] = y_t.T.astype(o_ref.dtype)


def dsa_attn(q, qi, proj, kv, kvt, gtab, wuv2, *, batch, seq):
    nb = seq // BLOCK
    topk = min(TOPK_MAX, seq // TOPK_DIV)
    kw_blk = 1792 // LANE
    g_off = seq - BLOCK
    return pl.pallas_call(
        functools.partial(_dsa_attn_kernel, topk, g_off),
        out_shape=jax.ShapeDtypeStruct((batch * seq, H_A * HEAD_DIM), jnp.bfloat16),
        grid=(batch, nb),
        in_specs=[
            pl.BlockSpec((H_A * BLOCK, KV_LORA), lambda b, n: (b * nb + n, 0)),
            pl.BlockSpec((BLOCK, IDX_HEADS * IDX_DIM), lambda b, n: (b * nb + n, 0)),
            pl.BlockSpec((BLOCK, LANE), lambda b, n: (b * nb + n, kw_blk)),
            pl.BlockSpec((seq, LANE), lambda b, n: (b, kw_blk)),
            pl.BlockSpec((seq, KV_LORA), lambda b, n: (b, 0)),
            pl.BlockSpec((None, KV_LORA, seq), lambda b, n: (b, 0, 0)),
            pl.BlockSpec(gtab.shape, lambda b, n: (0, 0), pipeline_mode=pl.Buffered(1)),
            pl.BlockSpec(wuv2.shape, lambda b, n: (0, 0, 0)),
        ],
        out_specs=pl.BlockSpec((BLOCK, H_A * HEAD_DIM), lambda b, n: (b * nb + n, 0)),
        scratch_shapes=[
            pltpu.VMEM((seq, LANE), jnp.bfloat16),
            pltpu.VMEM((seq, LANE), jnp.bfloat16),
            pltpu.VMEM((seq, BLOCK), jnp.int32),
            pltpu.VMEM((1, H_A * BLOCK), jnp.float32),
            pltpu.VMEM((1, H_A * BLOCK), jnp.float32),
            pltpu.VMEM((KV_LORA, H_A * BLOCK), jnp.float32),
        ],
        compiler_params=_cp(("parallel", "arbitrary")),
        name="dsa_attn",
    )(q, qi, proj, proj, kv, kvt, gtab, wuv2)


def _swa_kernel(q_ref, kc_ref, kp_ref, vc_ref, vp_ref, gq_ref, gk_ref, bias_ref, sink_ref, o_ref):
    n = pl.program_id(1)
    g = H_B // KV_B
    scale = HEAD_DIM ** -0.5
    kn = _pair_norm(jnp.concatenate([kp_ref[...], kc_ref[...]], axis=0), gk_ref[...])
    k_nat = kn.astype(jnp.bfloat16)
    k_swap = pltpu.roll(kn, HEAD_DIM, axis=1).astype(jnp.bfloat16)
    v = jnp.concatenate([vp_ref[...], vc_ref[...]], axis=0).astype(jnp.bfloat16)
    col = lax.broadcasted_iota(jnp.int32, (BLOCK, 2 * BLOCK), 1)
    first = jnp.where((n == 0) & (col < BLOCK), NEG, 0.0)
    lo = lax.broadcasted_iota(jnp.int32, (BLOCK, LANE), 1) < HEAD_DIM
    for p in range(H_B // 2):
        hk = (2 * p) // g
        qn = _pair_norm(q_ref[:, p * LANE:(p + 1) * LANE], gq_ref[...]) * scale
        halves = []
        for half in range(2):
            head = 2 * p + half
            qh = (jnp.where(lo, qn, 0.0) if half == 0 else jnp.where(lo, 0.0, qn)).astype(jnp.bfloat16)
            kh = k_nat if half == hk else k_swap
            s = lax.dot_general(qh, kh, _NT, preferred_element_type=jnp.float32) + bias_ref[head] + first
            sink = sink_ref[head]
            mx = jnp.maximum(jnp.max(s, axis=-1, keepdims=True), sink)
            e = jnp.exp(s - mx)
            den = jnp.sum(e, axis=-1, keepdims=True) + jnp.exp(sink - mx)
            o = jnp.dot((e / den).astype(jnp.bfloat16), v, preferred_element_type=jnp.float32)
            halves.append(o if half == hk else pltpu.roll(o, HEAD_DIM, axis=1))
        o_ref[:, p * LANE:(p + 1) * LANE] = jnp.where(lo, halves[0], halves[1]).astype(o_ref.dtype)


def swa_attn(proj, gq, gk, bias_tab, sinks, *, batch, seq):
    nb = seq // BLOCK
    k_blk = 1920 // LANE
    v_blk = 2048 // LANE
    prev = lambda b, n: b * nb + jnp.maximum(n - 1, 0)
    return pl.pallas_call(
        _swa_kernel,
        out_shape=jax.ShapeDtypeStruct((batch * seq, H_B * HEAD_DIM), jnp.bfloat16),
        grid=(batch, nb),
        in_specs=[
            pl.BlockSpec((BLOCK, H_B * HEAD_DIM), lambda b, n: (b * nb + n, 0)),
            pl.BlockSpec((BLOCK, LANE), lambda b, n: (b * nb + n, k_blk)),
            pl.BlockSpec((BLOCK, LANE), lambda b, n: (prev(b, n), k_blk)),
            pl.BlockSpec((BLOCK, LANE), lambda b, n: (b * nb + n, v_blk)),
            pl.BlockSpec((BLOCK, LANE), lambda b, n: (prev(b, n), v_blk)),
            pl.BlockSpec((1, LANE), lambda b, n: (0, 0)),
            pl.BlockSpec((1, LANE), lambda b, n: (0, 0)),
            pl.BlockSpec(bias_tab.shape, lambda b, n: (0, 0, 0)),
            pl.BlockSpec(memory_space=pltpu.SMEM),
        ],
        out_specs=pl.BlockSpec((BLOCK, H_B * HEAD_DIM), lambda b, n: (b * nb + n, 0)),
        compiler_params=_cp(("parallel", "parallel")),
        name="swa_attn",
    )(proj, proj, proj, proj, proj, jnp.tile(gq, 2).reshape(1, LANE), jnp.tile(gk, 2).reshape(1, LANE),
      bias_tab, sinks)


def _split3(c):
    hi = c.astype(jnp.bfloat16).astype(jnp.float32)
    r = c - hi
    mid = r.astype(jnp.bfloat16).astype(jnp.float32)
    return hi, mid, r - mid


def _fox_prep_kernel(q_ref, k_ref, v_ref, f_ref, gq_ref, gk_ref, fb_ref, qo_ref, ko_ref, vt_ref, carry_ref):
    tm = q_ref.shape[0]

    @pl.when(pl.program_id(1) == 0)
    def _():
        carry_ref[...] = jnp.zeros_like(carry_ref)

    z = f_ref[...] + fb_ref[...]
    logf = jnp.minimum(z, 0.0) - jnp.log1p(jnp.exp(-jnp.abs(z)))
    r = lax.broadcasted_iota(jnp.int32, (tm, tm), 0)
    c = lax.broadcasted_iota(jnp.int32, (tm, tm), 1)
    tri = jnp.where(c <= r, 1.0, 0.0)
    cum = jnp.dot(tri, logf, preferred_element_type=jnp.float32, precision=lax.Precision.HIGHEST) + carry_ref[...]
    carry_ref[...] = cum[tm - 1:tm, :]

    scale = HEAD_DIM ** -0.5
    lane = lax.broadcasted_iota(jnp.int32, (tm, LANE), 1)
    for p in range(H_C // 2):
        sl = slice(p * LANE, (p + 1) * LANE)
        qn = _pair_norm(q_ref[:, sl], gq_ref[...]) * scale
        kn = _pair_norm(k_ref[:, sl], gk_ref[...])
        for half in range(2):
            h = 2 * p + half
            qh = qn if half == 0 else pltpu.roll(qn, HEAD_DIM, axis=1)
            kh = kn if half == 0 else pltpu.roll(kn, HEAD_DIM, axis=1)
            hi, mid, lo = _split3(cum[:, h:h + 1])
            d = HEAD_DIM
            qa = jnp.where(lane < d, qh, jnp.where(lane == d, hi, jnp.where(lane == d + 1, mid, jnp.where(
                lane == d + 2, lo, jnp.where(lane < d + 6, 1.0, 0.0)))))
            ka = jnp.where(lane < d, kh, jnp.where(lane < d + 3, 1.0, jnp.where(lane == d + 3, -hi, jnp.where(
                lane == d + 4, -mid, jnp.where(lane == d + 5, -lo, 0.0)))))
            qo_ref[:, h * LANE:(h + 1) * LANE] = qa.astype(qo_ref.dtype)
            ko_ref[:, h * LANE:(h + 1) * LANE] = ka.astype(ko_ref.dtype)
    vt = v_ref[...].T
    ones = jnp.ones((V_ROWS - HEAD_DIM, tm), vt_ref.dtype)
    for h in range(H_C):
        vt_ref[h * V_ROWS:h * V_ROWS + HEAD_DIM, :] = vt[h * HEAD_DIM:(h + 1) * HEAD_DIM, :].astype(vt_ref.dtype)
        vt_ref[h * V_ROWS + HEAD_DIM:(h + 1) * V_ROWS, :] = ones


def fox_prep(proj, gq, gk, fbias, *, batch, seq, tm):
    m = proj.shape[0]
    hd = H_C * HEAD_DIM
    nt = seq // tm
    f_blk = 3 * hd // LANE
    fb = jnp.zeros((1, LANE), jnp.float32).at[0, :H_C].set(fbias)
    aug = jax.ShapeDtypeStruct((m, H_C * LANE), jnp.bfloat16)
    row = lambda b, i: b * nt + i
    return pl.pallas_call(
        _fox_prep_kernel,
        out_shape=(aug, aug, jax.ShapeDtypeStruct((batch, H_C * V_ROWS, seq), jnp.bfloat16)),
        grid=(batch, nt),
        in_specs=[
            pl.BlockSpec((tm, hd), lambda b, i: (row(b, i), 0)),
            pl.BlockSpec((tm, hd), lambda b, i: (row(b, i), 1)),
            pl.BlockSpec((tm, hd), lambda b, i: (row(b, i), 2)),
            pl.BlockSpec((tm, LANE), lambda b, i: (row(b, i), f_blk)),
            pl.BlockSpec((1, LANE), lambda b, i: (0, 0)),
            pl.BlockSpec((1, LANE), lambda b, i: (0, 0)),
            pl.BlockSpec((1, LANE), lambda b, i: (0, 0)),
        ],
        out_specs=(
            pl.BlockSpec((tm, H_C * LANE), lambda b, i: (row(b, i), 0)),
            pl.BlockSpec((tm, H_C * LANE), lambda b, i: (row(b, i), 0)),
            pl.BlockSpec((None, H_C * V_ROWS, tm), lambda b, i: (b, 0, i)),
        ),
        scratch_shapes=[pltpu.VMEM((1, LANE), jnp.float32)],
        compiler_params=_cp(("parallel", "arbitrary")),
        name="fox_prep",
    )(proj, proj, proj, proj, jnp.tile(gq, 2).reshape(1, LANE), jnp.tile(gk, 2).reshape(1, LANE), fb)


def _fox_attn_kernel(tq, q_ref, k_ref, vt_ref, o_ref, s_ref, p_ref):
    seq = q_ref.shape[0]
    diag_ok = (lax.broadcasted_iota(jnp.int32, (tq, tq), 0) <= lax.broadcasted_iota(jnp.int32, (tq, tq), 1))
    for i in range(seq // tq):
        nk = (i + 1) * tq
        outs = []
        for hh in range(2):
            hl = slice(hh * LANE, (hh + 1) * LANE)
            s_ref[hh, 0:nk, :] = lax.dot_general(k_ref[0:nk, hl], q_ref[i * tq:nk, hl], _NT,
                                                 preferred_element_type=jnp.float32)
            s_ref[hh, i * tq:nk, :] = jnp.where(diag_ok, s_ref[hh, i * tq:nk, :], NEG)
            m = jnp.max(s_ref[hh, 0:nk, :], axis=0, keepdims=True)
            p_ref[hh, 0:nk, :] = jnp.exp(s_ref[hh, 0:nk, :] - m).astype(p_ref.dtype)
            acc = jnp.dot(vt_ref[hh * V_ROWS:(hh + 1) * V_ROWS, 0:nk], p_ref[hh, 0:nk, :],
                          preferred_element_type=jnp.float32)
            outs.append(acc[:HEAD_DIM] / acc[HEAD_DIM:HEAD_DIM + 1])
        o_ref[i * tq:nk, :] = jnp.concatenate(outs, axis=0).T.astype(o_ref.dtype)


def fox_attn(qa, ka, vt, *, batch, seq, tq):
    pairs = H_C // 2
    return pl.pallas_call(
        functools.partial(_fox_attn_kernel, tq),
        out_shape=jax.ShapeDtypeStruct((batch * seq, H_C * HEAD_DIM), jnp.bfloat16),
        grid=(batch, pairs),
        in_specs=[
            pl.BlockSpec((seq, 2 * LANE), lambda b, c: (b, c)),
            pl.BlockSpec((seq, 2 * LANE), lambda b, c: (b, c)),
            pl.BlockSpec((None, 2 * V_ROWS, seq), lambda b, c: (b, c, 0)),
        ],
        out_specs=pl.BlockSpec((seq, 2 * HEAD_DIM), lambda b, c: (b, c)),
        scratch_shapes=[pltpu.VMEM((2, seq, tq), jnp.float32), pltpu.VMEM((2, seq, tq), jnp.bfloat16)],
        compiler_params=_cp(("parallel", "parallel")),
        name="fox_attn",
    )(qa, ka, vt)


def _rel_bucket(dist):
    n = jnp.maximum(dist, 0)
    exact = N_BUCKETS // 2
    nf = jnp.maximum(n, 1).astype(jnp.float32)
    large = exact + (jnp.log(nf / exact) / math.log(MAX_DISTANCE / exact) * (N_BUCKETS - exact)).astype(jnp.int32)
    large = jnp.minimum(large, N_BUCKETS - 1)
    return jnp.where(n < exact, n, large)


def _pad_cols(w, n):
    return jnp.pad(w, ((0, 0), (0, n - w.shape[1])))


def _toeplitz(v, rows, cols, off):
    h, dmax = v.shape
    lt = cols + rows - 1
    idx = off + rows - 1 - np.arange(lt)
    e = jnp.where(((idx >= 0) & (idx < dmax))[None], v[:, np.clip(idx, 0, dmax - 1)], 0.0)
    f = jnp.pad(e, ((0, 0), (0, 1)))
    a = jnp.tile(f, (1, rows))[:, :rows * lt].reshape(h, rows, lt)
    return a[:, :, rows - 1:rows - 1 + cols]


def kernel(x, p, attn_norm, ffn_norm, ple_norm, rel_bias, w_in_even, a_cq_norm, a_ckv_norm, a_w_uq, a_q_norm,
           a_w_qidx, a_w_uv, b_q_norm, b_k_norm, b_sinks, w_out_even, w_in_odd, c_forget_bias, c_q_norm,
           c_k_norm, w_out_odd, w_up, ffn_conv, w_down, w_ple_gate, w_ple_proj):
    batch, seq, d = x.shape
    depth = p.shape[0]
    m = batch * seq
    bf = jnp.bfloat16
    d_ff = w_down.shape[1]
    f_pad = -(-d_ff // 512) * 512
    hd_c = H_C * HEAD_DIM

    bias_by_dist = rel_bias[_rel_bucket(jnp.arange(seq))].T
    g_off = seq - BLOCK
    g_cols = g_off + KEY_TILE
    gtab = jnp.transpose(_toeplitz(bias_by_dist[:H_A], BLOCK, g_cols, g_off), (2, 0, 1)).reshape(g_cols, H_A * BLOCK)
    dist_b = np.arange(BLOCK)[:, None] + BLOCK - np.arange(2 * BLOCK)[None, :]
    band = (dist_b >= 0) & (dist_b < WINDOW)
    btab = jnp.where(band[None], _toeplitz(bias_by_dist[H_A:, :WINDOW], BLOCK, 2 * BLOCK, BLOCK), NEG)

    o1 = Q_LORA
    o2 = o1 + KV_LORA
    o3 = o2 + IDX_DIM
    o4 = o3 + IDX_HEADS
    o5 = o4 + H_B * HEAD_DIM
    o6 = o5 + KV_B * HEAD_DIM

    x = x.reshape(m, d)
    for i in range(depth):
        if i % 2 == 0:
            e = i // 2
            w = w_in_even[e]
            w_in = jnp.concatenate(
                [w[:, o4:o5], w[:, :o1], w[:, o1:o2], _pad_cols(w[:, o2:o4], LANE), w[:, o5:o6], w[:, o6:],
                 jnp.zeros((d, LANE), w.dtype)], axis=1).astype(bf)
            proj = norm_matmul(x, attn_norm[i], w_in, tm=1024, tn=w_in.shape[1] // 3)
            q, qi, kv, kvt = dsa_prep(proj, a_cq_norm[e], a_ckv_norm[e], a_q_norm[e], a_w_uq[e].astype(bf),
                                      a_w_qidx[e].astype(bf), seq=seq, tm=256)
            wuv = jnp.swapaxes(a_w_uv[e], 1, 2).reshape(H_A // 2, 2, HEAD_DIM, KV_LORA)
            zero = jnp.zeros_like(wuv[:, 0])
            wuv2 = jnp.concatenate([jnp.concatenate([wuv[:, 0], zero], axis=2),
                                    jnp.concatenate([zero, wuv[:, 1]], axis=2)], axis=1).astype(bf)
            y_a = dsa_attn(q, qi, proj, kv, kvt, gtab, wuv2, batch=batch, seq=seq)
            y_b = swa_attn(proj, b_q_norm[e], b_k_norm[e], btab, b_sinks[e], batch=batch, seq=seq)
            wo = w_out_even[e].astype(bf)
            x = matmul_res([(y_a, wo[:H_A * HEAD_DIM]), (y_b, wo[H_A * HEAD_DIM:])], x, tm=1024, tn=512)
        else:
            o = i // 2
            w_in = jnp.pad(w_in_odd[o], ((0, 0), (0, 3 * hd_c + LANE - w_in_odd[o].shape[1]))).astype(bf)
            proj = norm_matmul(x, attn_norm[i], w_in, tm=1024, tn=w_in.shape[1] // 7)
            qa, ka, vt = fox_prep(proj, c_q_norm[o], c_k_norm[o], c_forget_bias[o], batch=batch, seq=seq, tm=256)
            y = fox_attn(qa, ka, vt, batch=batch, seq=seq, tq=256)
            x = matmul_res([(y, w_out_odd[o].astype(bf))], x, tm=1024, tn=512)

        wg = _pad_cols(w_up[i][:, :d_ff], f_pad).astype(bf)
        wu = _pad_cols(w_up[i][:, d_ff:], f_pad).astype(bf)
        cg = _pad_cols(ffn_conv[i][:, :d_ff], f_pad)
        cu = _pad_cols(ffn_conv[i][:, d_ff:], f_pad)
        wd = jnp.pad(w_down[i], ((0, f_pad - d_ff), (0, 0))).astype(bf)
        x = conv_ffn(x, ffn_norm[i], wg, wu, cg, cu, wd, seq=seq, tm=512, tf=512)
        x = ple(x, ple_norm[i], p[i].reshape(m, -1), w_ple_gate[i].astype(bf), w_ple_proj[i].astype(bf),
                tm=1024, tn=512)
    return x.reshape(batch, seq, d)
```

```python
import functools
import math

import jax
import jax.numpy as jnp
import numpy as np
from jax import lax
from jax.experimental import pallas as pl
from jax.experimental.pallas import tpu as pltpu

HEAD_DIM = 64
BLOCK = 128
EPS = 1e-6
H_A = 16
Q_LORA = 512
KV_LORA = 256
IDX_HEADS = 16
IDX_DIM = 64
TOPK_MAX = 256
TOPK_DIV = 4
H_B = 16
KV_B = 2
WINDOW = 128
H_C = 32
N_BUCKETS = 32
MAX_DISTANCE = 1024
CONV_WIDTH = 3

LANE = 128
NEG = -1e30
KEY_TILE = 256
HALO = 16
FFN_CHUNK = 256
V_ROWS = HEAD_DIM + 16
VMEM_LIMIT = 56 * 1024 * 1024
INT_MIN = -2 ** 31

_NT = (((1,), (1,)), ((), ()))


def _cp(sem, vmem=VMEM_LIMIT):
    return pltpu.CompilerParams(dimension_semantics=sem, vmem_limit_bytes=vmem)


def _rms(x, g):
    return x * lax.rsqrt(jnp.mean(x * x, axis=-1, keepdims=True) + EPS) * g


def _pair_norm(x, g2):
    lo = lax.broadcasted_iota(jnp.int32, x.shape, 1) < HEAD_DIM
    x2 = x * x
    s_lo = jnp.sum(jnp.where(lo, x2, 0.0), axis=-1, keepdims=True)
    s_hi = jnp.sum(jnp.where(lo, 0.0, x2), axis=-1, keepdims=True)
    inv = jnp.where(lo, lax.rsqrt(s_lo / HEAD_DIM + EPS), lax.rsqrt(s_hi / HEAD_DIM + EPS))
    return x * inv * g2


def _norm_matmul_kernel(x_ref, g_ref, w_ref, o_ref, h_ref):
    @pl.when(pl.program_id(1) == 0)
    def _():
        h_ref[...] = _rms(x_ref[...], g_ref[...]).astype(h_ref.dtype)

    o_ref[...] = jnp.dot(h_ref[...], w_ref[...], preferred_element_type=jnp.float32)


def norm_matmul(x, g, w, *, tm, tn):
    m, d = x.shape
    n = w.shape[1]
    return pl.pallas_call(
        _norm_matmul_kernel,
        out_shape=jax.ShapeDtypeStruct((m, n), jnp.float32),
        grid=(m // tm, n // tn),
        in_specs=[
            pl.BlockSpec((tm, d), lambda i, j: (i, 0)),
            pl.BlockSpec((1, d), lambda i, j: (0, 0)),
            pl.BlockSpec((d, tn), lambda i, j: (0, j)),
        ],
        out_specs=pl.BlockSpec((tm, tn), lambda i, j: (i, j)),
        scratch_shapes=[pltpu.VMEM((tm, d), jnp.bfloat16)],
        compiler_params=_cp(("parallel", "arbitrary")),
        name="norm_matmul",
    )(x, g.reshape(1, d), w)


def _matmul_res_kernel(n_pairs, *refs):
    x_ref = refs[2 * n_pairs]
    o_ref = refs[2 * n_pairs + 1]
    acc = x_ref[...]
    for p in range(n_pairs):
        acc = acc + jnp.dot(refs[2 * p][...], refs[2 * p + 1][...], preferred_element_type=jnp.float32)
    o_ref[...] = acc


def matmul_res(pairs, x, *, tm, tn):
    m, n = x.shape
    in_specs, args = [], []
    for a, w in pairs:
        k = a.shape[1]
        in_specs += [pl.BlockSpec((tm, k), lambda i, j: (i, 0)), pl.BlockSpec((k, tn), lambda i, j: (0, j))]
        args += [a, w]
    in_specs.append(pl.BlockSpec((tm, tn), lambda i, j: (i, j)))
    return pl.pallas_call(
        functools.partial(_matmul_res_kernel, len(pairs)),
        out_shape=jax.ShapeDtypeStruct((m, n), jnp.float32),
        grid=(m // tm, n // tn),
        in_specs=in_specs,
        out_specs=pl.BlockSpec((tm, tn), lambda i, j: (i, j)),
        compiler_params=_cp(("parallel", "parallel")),
        name="matmul_res",
    )(*args, x)


def _ffn_kernel(tiles_per_seq, x_ref, xh_ref, g_ref, wg_ref, wu_ref, cg_ref, cu_ref, wd_ref, o_ref,
                h_ref, raw_ref, a_ref):
    i = pl.program_id(0)
    f = pl.program_id(1)

    @pl.when(f == 0)
    def _():
        seq_start = (i % tiles_per_seq) == 0
        hh = _rms(xh_ref[...], g_ref[...])
        h_ref[0:HALO, :] = jnp.where(seq_start, 0.0, hh).astype(h_ref.dtype)
        h_ref[HALO:, :] = _rms(x_ref[...], g_ref[...]).astype(h_ref.dtype)
        o_ref[...] = x_ref[...]

    def conv(z, c):
        z1 = pltpu.roll(z, 1, axis=0)
        z2 = pltpu.roll(z, 2, axis=0)
        y = c[2:3, :] * z + c[1:2, :] * z1 + c[0:1, :] * z2
        return y[HALO:, :]

    tf = wg_ref.shape[1]
    chunks = tf // FFN_CHUNK

    def up_proj(c):
        cs = slice(c * FFN_CHUNK, (c + 1) * FFN_CHUNK)
        raw_ref[c, 0] = jnp.dot(h_ref[...], wg_ref[:, cs], preferred_element_type=jnp.float32)
        raw_ref[c, 1] = jnp.dot(h_ref[...], wu_ref[:, cs], preferred_element_type=jnp.float32)

    def gate_down(c):
        cs = slice(c * FFN_CHUNK, (c + 1) * FFN_CHUNK)
        gate = conv(raw_ref[c, 0], cg_ref[:, cs])
        up = conv(raw_ref[c, 1], cu_ref[:, cs])
        a_ref[c] = (gate * jax.nn.sigmoid(gate) * up).astype(a_ref.dtype)
        o_ref[...] += jnp.dot(a_ref[c], wd_ref[cs, :], preferred_element_type=jnp.float32)

    up_proj(0)
    for c in range(chunks):
        if c + 1 < chunks:
            up_proj(c + 1)
        gate_down(c)


def conv_ffn(x, g, wg, wu, cg, cu, wd, *, seq, tm, tf):
    m, d = x.shape
    hb = tm // HALO
    return pl.pallas_call(
        functools.partial(_ffn_kernel, seq // tm),
        out_shape=jax.ShapeDtypeStruct((m, d), jnp.float32),
        grid=(m // tm, wg.shape[1] // tf),
        in_specs=[
            pl.BlockSpec((tm, d), lambda i, f: (i, 0)),
            pl.BlockSpec((HALO, d), lambda i, f: (jnp.maximum(i * hb - 1, 0), 0)),
            pl.BlockSpec((1, d), lambda i, f: (0, 0)),
            pl.BlockSpec((d, tf), lambda i, f: (0, f)),
            pl.BlockSpec((d, tf), lambda i, f: (0, f)),
            pl.BlockSpec((CONV_WIDTH, tf), lambda i, f: (0, f)),
            pl.BlockSpec((CONV_WIDTH, tf), lambda i, f: (0, f)),
            pl.BlockSpec((tf, d), lambda i, f: (f, 0)),
        ],
        out_specs=pl.BlockSpec((tm, d), lambda i, f: (i, 0)),
        scratch_shapes=[pltpu.VMEM((tm + HALO, d), jnp.bfloat16),
                        pltpu.VMEM((tf // FFN_CHUNK, 2, tm + HALO, FFN_CHUNK), jnp.float32),
                        pltpu.VMEM((tf // FFN_CHUNK, tm, FFN_CHUNK), jnp.bfloat16)],
        compiler_params=_cp(("parallel", "arbitrary")),
        name="conv_ffn",
    )(x, x, g.reshape(1, d), wg, wu, cg, cu, wd)


def _ple_kernel(x_ref, g_ref, p_ref, wg_ref, wp_ref, o_ref, h_ref):
    j = pl.program_id(1)
    tn = o_ref.shape[1]

    @pl.when(j == 0)
    def _():
        h_ref[...] = _rms(x_ref[...], g_ref[...]).astype(h_ref.dtype)

    gate = jax.nn.sigmoid(jnp.dot(h_ref[...], wg_ref[...], preferred_element_type=jnp.float32))
    pp = jnp.dot(p_ref[...].astype(jnp.bfloat16), wp_ref[...], preferred_element_type=jnp.float32)
    o_ref[...] = x_ref[:, pl.ds(pl.multiple_of(j * tn, LANE), tn)] + gate * pp


def ple(x, g, p, wg, wp, *, tm, tn):
    m, d = x.shape
    pd = p.shape[1]
    return pl.pallas_call(
        _ple_kernel,
        out_shape=jax.ShapeDtypeStruct((m, d), jnp.float32),
        grid=(m // tm, d // tn),
        in_specs=[
            pl.BlockSpec((tm, d), lambda i, j: (i, 0)),
            pl.BlockSpec((1, d), lambda i, j: (0, 0)),
            pl.BlockSpec((tm, pd), lambda i, j: (i, 0)),
            pl.BlockSpec((d, tn), lambda i, j: (0, j)),
            pl.BlockSpec((pd, tn), lambda i, j: (0, j)),
        ],
        out_specs=pl.BlockSpec((tm, tn), lambda i, j: (i, j)),
        scratch_shapes=[pltpu.VMEM((tm, d), jnp.bfloat16)],
        compiler_params=_cp(("parallel", "arbitrary")),
        name="ple",
    )(x, g.reshape(1, d), p, wg, wp)


def _dsa_prep_kernel(cq_ref, ckv_ref, gcq_ref, gckv_ref, gq_ref, wuq_ref, wqi_ref, q_ref, qi_ref, kv_ref, kvt_ref):
    tm = cq_ref.shape[0]
    cq = _rms(cq_ref[...], gcq_ref[...]).astype(jnp.bfloat16)
    kv = _rms(ckv_ref[...], gckv_ref[...])
    kv_ref[...] = kv.astype(kv_ref.dtype)
    kvt_ref[...] = kv.T.astype(kvt_ref.dtype)
    qi_ref[...] = jnp.dot(cq, wqi_ref[...], preferred_element_type=jnp.float32).astype(qi_ref.dtype)
    att_scale = KV_LORA ** -0.5
    for h in range(H_A):
        ql = jnp.dot(cq, wuq_ref[:, h * KV_LORA:(h + 1) * KV_LORA], preferred_element_type=jnp.float32)
        qn = (_rms(ql, gq_ref[...]) * att_scale).astype(q_ref.dtype)
        for r in range(tm // BLOCK):
            q_ref[(r * H_A + h) * BLOCK:(r * H_A + h + 1) * BLOCK, :] = qn[r * BLOCK:(r + 1) * BLOCK, :]


def dsa_prep(proj, gcq, gckv, gq, wuq, wqi, *, seq, tm):
    m = proj.shape[0]
    cq_blk = 1024 // Q_LORA
    ckv_blk = 1536 // KV_LORA
    nt = seq // tm
    return pl.pallas_call(
        _dsa_prep_kernel,
        out_shape=(
            jax.ShapeDtypeStruct((m * H_A, KV_LORA), jnp.bfloat16),
            jax.ShapeDtypeStruct((m, IDX_HEADS * IDX_DIM), jnp.bfloat16),
            jax.ShapeDtypeStruct((m, KV_LORA), jnp.bfloat16),
            jax.ShapeDtypeStruct((m // seq, KV_LORA, seq), jnp.bfloat16),
        ),
        grid=(m // tm,),
        in_specs=[
            pl.BlockSpec((tm, Q_LORA), lambda i: (i, cq_blk)),
            pl.BlockSpec((tm, KV_LORA), lambda i: (i, ckv_blk)),
            pl.BlockSpec((1, Q_LORA), lambda i: (0, 0)),
            pl.BlockSpec((1, KV_LORA), lambda i: (0, 0)),
            pl.BlockSpec((1, KV_LORA), lambda i: (0, 0)),
            pl.BlockSpec((Q_LORA, H_A * KV_LORA), lambda i: (0, 0)),
            pl.BlockSpec((Q_LORA, IDX_HEADS * IDX_DIM), lambda i: (0, 0)),
        ],
        out_specs=(
            pl.BlockSpec((tm * H_A, KV_LORA), lambda i: (i, 0)),
            pl.BlockSpec((tm, IDX_HEADS * IDX_DIM), lambda i: (i, 0)),
            pl.BlockSpec((tm, KV_LORA), lambda i: (i, 0)),
            pl.BlockSpec((None, KV_LORA, tm), lambda i: (i // nt, 0, i % nt)),
        ),
        compiler_params=_cp(("parallel",)),
        name="dsa_prep",
    )(proj, proj, gcq.reshape(1, -1), gckv.reshape(1, -1), gq.reshape(1, -1), wuq, wqi)


def _dsa_attn_kernel(topk, g_off, q_ref, qi_ref, kwq_ref, kwall_ref, kv_ref, kvt_ref, g_ref, wuv_ref, o_ref,
                     ka_ref, kb_ref, key_ref, m_ref, l_ref, acc_ref):
    n = pl.program_id(1)
    n_tiles = (n + 2) // 2
    q_pos = n * BLOCK + lax.broadcasted_iota(jnp.int32, (KEY_TILE, BLOCK), 1)
    k_row = lax.broadcasted_iota(jnp.int32, (KEY_TILE, BLOCK), 0)

    @pl.when(n == 0)
    def _():
        kw = kwall_ref[...]
        lo = lax.broadcasted_iota(jnp.int32, kw.shape, 1) < IDX_DIM
        ka_ref[...] = jnp.where(lo, kw, 0.0).astype(ka_ref.dtype)
        kb_ref[...] = jnp.where(lo, 0.0, pltpu.roll(kw, IDX_DIM, axis=1)).astype(kb_ref.dtype)

    w_t = kwq_ref[...].T

    def score_tile(j, carry):
        col = pl.multiple_of(j * KEY_TILE, KEY_TILE)
        ka = ka_ref[pl.ds(col, KEY_TILE), :]
        kb = kb_ref[pl.ds(col, KEY_TILE), :]
        s = jnp.zeros((KEY_TILE, BLOCK), jnp.float32)
        for p in range(IDX_HEADS // 2):
            qp = qi_ref[:, p * LANE:(p + 1) * LANE]
            da = lax.dot_general(ka, qp, _NT, preferred_element_type=jnp.float32)
            db = lax.dot_general(kb, qp, _NT, preferred_element_type=jnp.float32)
            s = s + w_t[IDX_DIM + 2 * p:IDX_DIM + 2 * p + 1, :] * jnp.maximum(da, 0.0)
            s = s + w_t[IDX_DIM + 2 * p + 1:IDX_DIM + 2 * p + 2, :] * jnp.maximum(db, 0.0)
        s = jnp.where(col + k_row <= q_pos, s, -jnp.inf)
        bits = pltpu.bitcast(s, jnp.int32)
        key_ref[pl.ds(col, KEY_TILE), :] = jnp.where(bits < 0, bits ^ jnp.int32(0x7FFFFFFF), bits)
        return carry

    lax.fori_loop(0, n_tiles, score_tile, 0)

    def bit_step(b, tu):
        cand_u = tu | lax.shift_left(jnp.int32(1), jnp.int32(31) - b)
        cand_s = cand_u ^ jnp.int32(INT_MIN)

        def count_tile(j, cnt):
            col = pl.multiple_of(j * KEY_TILE, KEY_TILE)
            hit = jnp.where(key_ref[pl.ds(col, KEY_TILE), :] >= cand_s, 1, 0)
            return cnt + jnp.sum(hit.reshape(KEY_TILE // 8, 8, BLOCK), axis=0)

        cnt = lax.fori_loop(0, n_tiles, count_tile, jnp.zeros((8, BLOCK), jnp.int32))
        total = jnp.sum(cnt, axis=0, keepdims=True)
        return jnp.where(total >= topk, cand_u, tu)

    thr = lax.fori_loop(0, 32, bit_step, jnp.zeros((1, BLOCK), jnp.int32)) ^ jnp.int32(INT_MIN)

    m_ref[...] = jnp.full_like(m_ref, NEG)
    l_ref[...] = jnp.zeros_like(l_ref)
    acc_ref[...] = jnp.zeros_like(acc_ref)

    def attn_tile(j, carry):
        col = pl.multiple_of(j * KEY_TILE, KEY_TILE)
        sel = (key_ref[pl.ds(col, KEY_TILE), :] >= thr) & (col + k_row <= q_pos)
        am = jnp.where(sel, 0.0, NEG)
        s = lax.dot_general(kv_ref[pl.ds(col, KEY_TILE), :], q_ref[...], _NT, preferred_element_type=jnp.float32)
        grow = pl.multiple_of(g_off - n * BLOCK + col, LANE)
        s = s + (g_ref[pl.ds(grow, KEY_TILE), :] + jnp.concatenate([am] * H_A, axis=1))
        m_prev = m_ref[...]
        m_new = jnp.maximum(m_prev, jnp.max(s, axis=0, keepdims=True))
        alpha = jnp.exp(m_prev - m_new)
        p = jnp.exp(s - m_new)
        l_ref[...] = alpha * l_ref[...] + jnp.sum(p, axis=0, keepdims=True)
        acc_ref[...] = alpha * acc_ref[...] + jnp.dot(
            kvt_ref[:, pl.ds(col, KEY_TILE)], p.astype(jnp.bfloat16), preferred_element_type=jnp.float32)
        m_ref[...] = m_new
        return carry

    lax.fori_loop(0, n_tiles, attn_tile, 0)

    o_t = (acc_ref[...] / l_ref[...]).astype(jnp.bfloat16)
    for p in range(H_A // 2):
        pair = jnp.concatenate([o_t[:, (2 * p) * BLOCK:(2 * p + 1) * BLOCK],
                                o_t[:, (2 * p + 1) * BLOCK:(2 * p + 2) * BLOCK]], axis=0)
        y_t = jnp.dot(wuv_ref[p], pair, preferred_element_type=jnp.float32)
        o_ref[:, p * LANE:(p + 1) * LANE] = y_t.T.astype(o_ref.dtype)


def dsa_attn(q, qi, proj, kv, kvt, gtab, wuv2, *, batch, seq):
    nb = seq // BLOCK
    topk = min(TOPK_MAX, seq // TOPK_DIV)
    kw_blk = 1792 // LANE
    g_off = seq - BLOCK
    return pl.pallas_call(
        functools.partial(_dsa_attn_kernel, topk, g_off),
        out_shape=jax.ShapeDtypeStruct((batch * seq, H_A * HEAD_DIM), jnp.bfloat16),
        grid=(batch, nb),
        in_specs=[
            pl.BlockSpec((H_A * BLOCK, KV_LORA), lambda b, n: (b * nb + n, 0)),
            pl.BlockSpec((BLOCK, IDX_HEADS * IDX_DIM), lambda b, n: (b * nb + n, 0)),
            pl.BlockSpec((BLOCK, LANE), lambda b, n: (b * nb + n, kw_blk)),
            pl.BlockSpec((seq, LANE), lambda b, n: (b, kw_blk)),
            pl.BlockSpec((seq, KV_LORA), lambda b, n: (b, 0)),
            pl.BlockSpec((None, KV_LORA, seq), lambda b, n: (b, 0, 0)),
            pl.BlockSpec(gtab.shape, lambda b, n: (0, 0), pipeline_mode=pl.Buffered(1)),
            pl.BlockSpec(wuv2.shape, lambda b, n: (0, 0, 0)),
        ],
        out_specs=pl.BlockSpec((BLOCK, H_A * HEAD_DIM), lambda b, n: (b * nb + n, 0)),
        scratch_shapes=[
            pltpu.VMEM((seq, LANE), jnp.bfloat16),
            pltpu.VMEM((seq, LANE), jnp.bfloat16),
            pltpu.VMEM((seq, BLOCK), jnp.int32),
            pltpu.VMEM((1, H_A * BLOCK), jnp.float32),
            pltpu.VMEM((1, H_A * BLOCK), jnp.float32),
            pltpu.VMEM((KV_LORA, H_A * BLOCK), jnp.float32),
        ],
        compiler_params=_cp(("parallel", "arbitrary")),
        name="dsa_attn",
    )(q, qi, proj, proj, kv, kvt, gtab, wuv2)


def _swa_kernel(q_ref, kc_ref, kp_ref, vc_ref, vp_ref, gq_ref, gk_ref, bias_ref, sink_ref, o_ref):
    n = pl.program_id(1)
    g = H_B // KV_B
    scale = HEAD_DIM ** -0.5
    kn = _pair_norm(jnp.concatenate([kp_ref[...], kc_ref[...]], axis=0), gk_ref[...])
    k_nat = kn.astype(jnp.bfloat16)
    k_swap = pltpu.roll(kn, HEAD_DIM, axis=1).astype(jnp.bfloat16)
    v = jnp.concatenate([vp_ref[...], vc_ref[...]], axis=0).astype(jnp.bfloat16)
    col = lax.broadcasted_iota(jnp.int32, (BLOCK, 2 * BLOCK), 1)
    first = jnp.where((n == 0) & (col < BLOCK), NEG, 0.0)
    lo = lax.broadcasted_iota(jnp.int32, (BLOCK, LANE), 1) < HEAD_DIM
    for p in range(H_B // 2):
        hk = (2 * p) // g
        qn = _pair_norm(q_ref[:, p * LANE:(p + 1) * LANE], gq_ref[...]) * scale
        halves = []
        for half in range(2):
            head = 2 * p + half
            qh = (jnp.where(lo, qn, 0.0) if half == 0 else jnp.where(lo, 0.0, qn)).astype(jnp.bfloat16)
            kh = k_nat if half == hk else k_swap
            s = lax.dot_general(qh, kh, _NT, preferred_element_type=jnp.float32) + bias_ref[head] + first
            sink = sink_ref[head]
            mx = jnp.maximum(jnp.max(s, axis=-1, keepdims=True), sink)
            e = jnp.exp(s - mx)
            den = jnp.sum(e, axis=-1, keepdims=True) + jnp.exp(sink - mx)
            o = jnp.dot((e / den).astype(jnp.bfloat16), v, preferred_element_type=jnp.float32)
            halves.append(o if half == hk else pltpu.roll(o, HEAD_DIM, axis=1))
        o_ref[:, p * LANE:(p + 1) * LANE] = jnp.where(lo, halves[0], halves[1]).astype(o_ref.dtype)


def swa_attn(proj, gq, gk, bias_tab, sinks, *, batch, seq):
    nb = seq // BLOCK
    k_blk = 1920 // LANE
    v_blk = 2048 // LANE
    prev = lambda b, n: b * nb + jnp.maximum(n - 1, 0)
    return pl.pallas_call(
        _swa_kernel,
        out_shape=jax.ShapeDtypeStruct((batch * seq, H_B * HEAD_DIM), jnp.bfloat16),
        grid=(batch, nb),
        in_specs=[
            pl.BlockSpec((BLOCK, H_B * HEAD_DIM), lambda b, n: (b * nb + n, 0)),
            pl.BlockSpec((BLOCK, LANE), lambda b, n: (b * nb + n, k_blk)),
            pl.BlockSpec((BLOCK, LANE), lambda b, n: (prev(b, n), k_blk)),
            pl.BlockSpec((BLOCK, LANE), lambda b, n: (b * nb + n, v_blk)),
            pl.BlockSpec((BLOCK, LANE), lambda b, n: (prev(b, n), v_blk)),
            pl.BlockSpec((1, LANE), lambda b, n: (0, 0)),
            pl.BlockSpec((1, LANE), lambda b, n: (0, 0)),
            pl.BlockSpec(bias_tab.shape, lambda b, n: (0, 0, 0)),
            pl.BlockSpec(memory_space=pltpu.SMEM),
        ],
        out_specs=pl.BlockSpec((BLOCK, H_B * HEAD_DIM), lambda b, n: (b * nb + n, 0)),
        compiler_params=_cp(("parallel", "parallel")),
        name="swa_attn",
    )(proj, proj, proj, proj, proj, jnp.tile(gq, 2).reshape(1, LANE), jnp.tile(gk, 2).reshape(1, LANE),
      bias_tab, sinks)


def _split3(c):
    hi = c.astype(jnp.bfloat16).astype(jnp.float32)
    r = c - hi
    mid = r.astype(jnp.bfloat16).astype(jnp.float32)
    return hi, mid, r - mid


def _fox_prep_kernel(q_ref, k_ref, v_ref, f_ref, gq_ref, gk_ref, fb_ref, qo_ref, ko_ref, vt_ref, carry_ref):
    tm = q_ref.shape[0]

    @pl.when(pl.program_id(1) == 0)
    def _():
        carry_ref[...] = jnp.zeros_like(carry_ref)

    z = f_ref[...] + fb_ref[...]
    logf = jnp.minimum(z, 0.0) - jnp.log1p(jnp.exp(-jnp.abs(z)))
    r = lax.broadcasted_iota(jnp.int32, (tm, tm), 0)
    c = lax.broadcasted_iota(jnp.int32, (tm, tm), 1)
    tri = jnp.where(c <= r, 1.0, 0.0)
    cum = jnp.dot(tri, logf, preferred_element_type=jnp.float32, precision=lax.Precision.HIGHEST) + carry_ref[...]
    carry_ref[...] = cum[tm - 1:tm, :]

    scale = HEAD_DIM ** -0.5
    lane = lax.broadcasted_iota(jnp.int32, (tm, LANE), 1)
    for p in range(H_C // 2):
        sl = slice(p * LANE, (p + 1) * LANE)
        qn = _pair_norm(q_ref[:, sl], gq_ref[...]) * scale
        kn = _pair_norm(k_ref[:, sl], gk_ref[...])
        for half in range(2):
            h = 2 * p + half
            qh = qn if half == 0 else pltpu.roll(qn, HEAD_DIM, axis=1)
            kh = kn if half == 0 else pltpu.roll(kn, HEAD_DIM, axis=1)
            hi, mid, lo = _split3(cum[:, h:h + 1])
            d = HEAD_DIM
            qa = jnp.where(lane < d, qh, jnp.where(lane == d, hi, jnp.where(lane == d + 1, mid, jnp.where(
                lane == d + 2, lo, jnp.where(lane < d + 6, 1.0, 0.0)))))
            ka = jnp.where(lane < d, kh, jnp.where(lane < d + 3, 1.0, jnp.where(lane == d + 3, -hi, jnp.where(
                lane == d + 4, -mid, jnp.where(lane == d + 5, -lo, 0.0)))))
            qo_ref[:, h * LANE:(h + 1) * LANE] = qa.astype(qo_ref.dtype)
            ko_ref[:, h * LANE:(h + 1) * LANE] = ka.astype(ko_ref.dtype)
    vt = v_ref[...].T
    ones = jnp.ones((V_ROWS - HEAD_DIM, tm), vt_ref.dtype)
    for h in range(H_C):
        vt_ref[h * V_ROWS:h * V_ROWS + HEAD_DIM, :] = vt[h * HEAD_DIM:(h + 1) * HEAD_DIM, :].astype(vt_ref.dtype)
        vt_ref[h * V_ROWS + HEAD_DIM:(h + 1) * V_ROWS, :] = ones


def fox_prep(proj, gq, gk, fbias, *, batch, seq, tm):
    m = proj.shape[0]
    hd = H_C * HEAD_DIM
    nt = seq // tm
    f_blk = 3 * hd // LANE
    fb = jnp.zeros((1, LANE), jnp.float32).at[0, :H_C].set(fbias)
    aug = jax.ShapeDtypeStruct((m, H_C * LANE), jnp.bfloat16)
    row = lambda b, i: b * nt + i
    return pl.pallas_call(
        _fox_prep_kernel,
        out_shape=(aug, aug, jax.ShapeDtypeStruct((batch, H_C * V_ROWS, seq), jnp.bfloat16)),
        grid=(batch, nt),
        in_specs=[
            pl.BlockSpec((tm, hd), lambda b, i: (row(b, i), 0)),
            pl.BlockSpec((tm, hd), lambda b, i: (row(b, i), 1)),
            pl.BlockSpec((tm, hd), lambda b, i: (row(b, i), 2)),
            pl.BlockSpec((tm, LANE), lambda b, i: (row(b, i), f_blk)),
            pl.BlockSpec((1, LANE), lambda b, i: (0, 0)),
            pl.BlockSpec((1, LANE), lambda b, i: (0, 0)),
            pl.BlockSpec((1, LANE), lambda b, i: (0, 0)),
        ],
        out_specs=(
            pl.BlockSpec((tm, H_C * LANE), lambda b, i: (row(b, i), 0)),
            pl.BlockSpec((tm, H_C * LANE), lambda b, i: (row(b, i), 0)),
            pl.BlockSpec((None, H_C * V_ROWS, tm), lambda b, i: (b, 0, i)),
        ),
        scratch_shapes=[pltpu.VMEM((1, LANE), jnp.float32)],
        compiler_params=_cp(("parallel", "arbitrary")),
        name="fox_prep",
    )(proj, proj, proj, proj, jnp.tile(gq, 2).reshape(1, LANE), jnp.tile(gk, 2).reshape(1, LANE), fb)


def _fox_attn_kernel(tq, q_ref, k_ref, vt_ref, o_ref, s_ref, p_ref):
    seq = q_ref.shape[0]
    nq = seq // tq
    diag_ok = (lax.broadcasted_iota(jnp.int32, (tq, tq), 0) <= lax.broadcasted_iota(jnp.int32, (tq, tq), 1))

    def logits(i):
        nk = (i + 1) * tq
        for hh in range(2):
            hl = slice(hh * LANE, (hh + 1) * LANE)
            s = lax.dot_general(k_ref[0:nk, hl], q_ref[i * tq:nk, hl], _NT,
                                preferred_element_type=jnp.float32)
            if i > 0:
                s_ref[i % 2, hh, 0:i * tq, :] = s[:i * tq]
            s_ref[i % 2, hh, i * tq:nk, :] = jnp.where(diag_ok, s[i * tq:], NEG)

    def probs(i):
        nk = (i + 1) * tq
        for hh in range(2):
            s = s_ref[i % 2, hh, 0:nk, :]
            p_ref[i % 2, hh, 0:nk, :] = jnp.exp(s - jnp.max(s, axis=0, keepdims=True)).astype(p_ref.dtype)

    def values(i):
        nk = (i + 1) * tq
        outs = []
        for hh in range(2):
            acc = jnp.dot(vt_ref[hh * V_ROWS:(hh + 1) * V_ROWS, 0:nk], p_ref[i % 2, hh, 0:nk, :],
                          preferred_element_type=jnp.float32)
            outs.append(acc[:HEAD_DIM] / acc[HEAD_DIM:HEAD_DIM + 1])
        o_ref[i * tq:nk, :] = jnp.concatenate(outs, axis=0).T.astype(o_ref.dtype)

    logits(0)
    for i in range(nq):
        if i > 0:
            values(i - 1)
        if i + 1 < nq:
            logits(i + 1)
        probs(i)
    values(nq - 1)


def fox_attn(qa, ka, vt, *, batch, seq, tq):
    pairs = H_C // 2
    return pl.pallas_call(
        functools.partial(_fox_attn_kernel, tq),
        out_shape=jax.ShapeDtypeStruct((batch * seq, H_C * HEAD_DIM), jnp.bfloat16),
        grid=(batch, pairs),
        in_specs=[
            pl.BlockSpec((seq, 2 * LANE), lambda b, c: (b, c)),
            pl.BlockSpec((seq, 2 * LANE), lambda b, c: (b, c)),
            pl.BlockSpec((None, 2 * V_ROWS, seq), lambda b, c: (b, c, 0)),
        ],
        out_specs=pl.BlockSpec((seq, 2 * HEAD_DIM), lambda b, c: (b, c)),
        scratch_shapes=[pltpu.VMEM((2, 2, seq, tq), jnp.float32), pltpu.VMEM((2, 2, seq, tq), jnp.bfloat16)],
        compiler_params=_cp(("parallel", "parallel")),
        name="fox_attn",
    )(qa, ka, vt)


def _rel_bucket(dist):
    n = jnp.maximum(dist, 0)
    exact = N_BUCKETS // 2
    nf = jnp.maximum(n, 1).astype(jnp.float32)
    large = exact + (jnp.log(nf / exact) / math.log(MAX_DISTANCE / exact) * (N_BUCKETS - exact)).astype(jnp.int32)
    large = jnp.minimum(large, N_BUCKETS - 1)
    return jnp.where(n < exact, n, large)


def _pad_cols(w, n):
    return jnp.pad(w, ((0, 0), (0, n - w.shape[1])))


def _toeplitz(v, rows, cols, off):
    h, dmax = v.shape
    lt = cols + rows - 1
    idx = off + rows - 1 - np.arange(lt)
    e = jnp.where(((idx >= 0) & (idx < dmax))[None], v[:, np.clip(idx, 0, dmax - 1)], 0.0)
    f = jnp.pad(e, ((0, 0), (0, 1)))
    a = jnp.tile(f, (1, rows))[:, :rows * lt].reshape(h, rows, lt)
    return a[:, :, rows - 1:rows - 1 + cols]


def kernel(x, p, attn_norm, ffn_norm, ple_norm, rel_bias, w_in_even, a_cq_norm, a_ckv_norm, a_w_uq, a_q_norm,
           a_w_qidx, a_w_uv, b_q_norm, b_k_norm, b_sinks, w_out_even, w_in_odd, c_forget_bias, c_q_norm,
           c_k_norm, w_out_odd, w_up, ffn_conv, w_down, w_ple_gate, w_ple_proj):
    batch, seq, d = x.shape
    depth = p.shape[0]
    m = batch * seq
    bf = jnp.bfloat16
    d_ff = w_down.shape[1]
    f_pad = -(-d_ff // 512) * 512
    hd_c = H_C * HEAD_DIM

    bias_by_dist = rel_bias[_rel_bucket(jnp.arange(seq))].T
    g_off = seq - BLOCK
    g_cols = g_off + KEY_TILE
    gtab = jnp.transpose(_toeplitz(bias_by_dist[:H_A], BLOCK, g_cols, g_off), (2, 0, 1)).reshape(g_cols, H_A * BLOCK)
    dist_b = np.arange(BLOCK)[:, None] + BLOCK - np.arange(2 * BLOCK)[None, :]
    band = (dist_b >= 0) & (dist_b < WINDOW)
    btab = jnp.where(band[None], _toeplitz(bias_by_dist[H_A:, :WINDOW], BLOCK, 2 * BLOCK, BLOCK), NEG)

    o1 = Q_LORA
    o2 = o1 + KV_LORA
    o3 = o2 + IDX_DIM
    o4 = o3 + IDX_HEADS
    o5 = o4 + H_B * HEAD_DIM
    o6 = o5 + KV_B * HEAD_DIM

    x = x.reshape(m, d)
    for i in range(depth):
        if i % 2 == 0:
            e = i // 2
            w = w_in_even[e]
            w_in = jnp.concatenate(
                [w[:, o4:o5], w[:, :o1], w[:, o1:o2], _pad_cols(w[:, o2:o4], LANE), w[:, o5:o6], w[:, o6:],
                 jnp.zeros((d, LANE), w.dtype)], axis=1).astype(bf)
            proj = norm_matmul(x, attn_norm[i], w_in, tm=1024, tn=w_in.shape[1] // 3)
            q, qi, kv, kvt = dsa_prep(proj, a_cq_norm[e], a_ckv_norm[e], a_q_norm[e], a_w_uq[e].astype(bf),
                                      a_w_qidx[e].astype(bf), seq=seq, tm=256)
            wuv = jnp.swapaxes(a_w_uv[e], 1, 2).reshape(H_A // 2, 2, HEAD_DIM, KV_LORA)
            zero = jnp.zeros_like(wuv[:, 0])
            wuv2 = jnp.concatenate([jnp.concatenate([wuv[:, 0], zero], axis=2),
                                    jnp.concatenate([zero, wuv[:, 1]], axis=2)], axis=1).astype(bf)
            y_a = dsa_attn(q, qi, proj, kv, kvt, gtab, wuv2, batch=batch, seq=seq)
            y_b = swa_attn(proj, b_q_norm[e], b_k_norm[e], btab, b_sinks[e], batch=batch, seq=seq)
            wo = w_out_even[e].astype(bf)
            x = matmul_res([(y_a, wo[:H_A * HEAD_DIM]), (y_b, wo[H_A * HEAD_DIM:])], x, tm=1024, tn=512)
        else:
            o = i // 2
            w_in = jnp.pad(w_in_odd[o], ((0, 0), (0, 3 * hd_c + LANE - w_in_odd[o].shape[1]))).astype(bf)
            proj = norm_matmul(x, attn_norm[i], w_in, tm=1024, tn=w_in.shape[1] // 7)
            qa, ka, vt = fox_prep(proj, c_q_norm[o], c_k_norm[o], c_forget_bias[o], batch=batch, seq=seq, tm=256)
            y = fox_attn(qa, ka, vt, batch=batch, seq=seq, tq=256)
            x = matmul_res([(y, w_out_odd[o].astype(bf))], x, tm=1024, tn=512)

        wg = _pad_cols(w_up[i][:, :d_ff], f_pad).astype(bf)
        wu = _pad_cols(w_up[i][:, d_ff:], f_pad).astype(bf)
        cg = _pad_cols(ffn_conv[i][:, :d_ff], f_pad)
        cu = _pad_cols(ffn_conv[i][:, d_ff:], f_pad)
        wd = jnp.pad(w_down[i], ((0, f_pad - d_ff), (0, 0))).astype(bf)
        x = conv_ffn(x, ffn_norm[i], wg, wu, cg, cu, wd, seq=seq, tm=512, tf=512)
        x = ple(x, ple_norm[i], p[i].reshape(m, -1), w_ple_gate[i].astype(bf), w_ple_proj[i].astype(bf),
                tm=1024, tn=512)
    return x.reshape(batch, seq, d)
```

```python
import functools
import math

import jax
import jax.numpy as jnp
import numpy as np
from jax import lax
from jax.experimental import pallas as pl
from jax.experimental.pallas import tpu as pltpu

HEAD_DIM = 64
BLOCK = 128
EPS = 1e-6
H_A = 16
Q_LORA = 512
KV_LORA = 256
IDX_HEADS = 16
IDX_DIM = 64
TOPK_MAX = 256
TOPK_DIV = 4
H_B = 16
KV_B = 2
WINDOW = 128
H_C = 32
N_BUCKETS = 32
MAX_DISTANCE = 1024
CONV_WIDTH = 3

LANE = 128
NEG = -1e30
KEY_TILE = 512
HALO = 16
DSA_GROUPS = 2
KVT_ROWS = KV_LORA + 16
V_ROWS = HEAD_DIM + 16
VMEM_LIMIT = 56 * 1024 * 1024
INT_MIN = -2 ** 31

_NT = (((1,), (1,)), ((), ()))


def _cp(sem, vmem=VMEM_LIMIT):
    return pltpu.CompilerParams(dimension_semantics=sem, vmem_limit_bytes=vmem)


def _rms(x, g):
    return x * lax.rsqrt(jnp.mean(x * x, axis=-1, keepdims=True) + EPS) * g


def _pair_norm(x, g2):
    lo = lax.broadcasted_iota(jnp.int32, x.shape, 1) < HEAD_DIM
    x2 = x * x
    s_lo = jnp.sum(jnp.where(lo, x2, 0.0), axis=-1, keepdims=True)
    s_hi = jnp.sum(jnp.where(lo, 0.0, x2), axis=-1, keepdims=True)
    inv = jnp.where(lo, lax.rsqrt(s_lo / HEAD_DIM + EPS), lax.rsqrt(s_hi / HEAD_DIM + EPS))
    return x * inv * g2


def _norm_matmul_kernel(x_ref, g_ref, w_ref, o_ref, h_ref):
    @pl.when(pl.program_id(1) == 0)
    def _():
        h_ref[...] = _rms(x_ref[...], g_ref[...]).astype(h_ref.dtype)

    o_ref[...] = jnp.dot(h_ref[...], w_ref[...], preferred_element_type=jnp.float32)


def norm_matmul(x, g, w, *, tm, tn):
    m, d = x.shape
    n = w.shape[1]
    return pl.pallas_call(
        _norm_matmul_kernel,
        out_shape=jax.ShapeDtypeStruct((m, n), jnp.float32),
        grid=(m // tm, n // tn),
        in_specs=[
            pl.BlockSpec((tm, d), lambda i, j: (i, 0)),
            pl.BlockSpec((1, d), lambda i, j: (0, 0)),
            pl.BlockSpec((d, tn), lambda i, j: (0, j)),
        ],
        out_specs=pl.BlockSpec((tm, tn), lambda i, j: (i, j)),
        scratch_shapes=[pltpu.VMEM((tm, d), jnp.bfloat16)],
        compiler_params=_cp(("parallel", "arbitrary")),
        name="norm_matmul",
    )(x, g.reshape(1, d), w)


def _matmul_res_kernel(n_pairs, *refs):
    x_ref = refs[2 * n_pairs]
    o_ref = refs[2 * n_pairs + 1]
    acc = x_ref[...]
    for p in range(n_pairs):
        acc = acc + jnp.dot(refs[2 * p][...], refs[2 * p + 1][...], preferred_element_type=jnp.float32)
    o_ref[...] = acc


def matmul_res(pairs, x, *, tm, tn):
    m, n = x.shape
    in_specs, args = [], []
    for a, w in pairs:
        k = a.shape[1]
        in_specs += [pl.BlockSpec((tm, k), lambda i, j: (i, 0)), pl.BlockSpec((k, tn), lambda i, j: (0, j))]
        args += [a, w]
    in_specs.append(pl.BlockSpec((tm, tn), lambda i, j: (i, j)))
    return pl.pallas_call(
        functools.partial(_matmul_res_kernel, len(pairs)),
        out_shape=jax.ShapeDtypeStruct((m, n), jnp.float32),
        grid=(m // tm, n // tn),
        in_specs=in_specs,
        out_specs=pl.BlockSpec((tm, tn), lambda i, j: (i, j)),
        compiler_params=_cp(("parallel", "parallel")),
        name="matmul_res",
    )(*args, x)


def _ffn_kernel(tiles_per_seq, x_ref, xh_ref, g_ref, wg_ref, wu_ref, cg_ref, cu_ref, wd_ref, o_ref, h_ref):
    i = pl.program_id(0)
    f = pl.program_id(1)

    @pl.when(f == 0)
    def _():
        seq_start = (i % tiles_per_seq) == 0
        hh = _rms(xh_ref[...], g_ref[...])
        h_ref[0:HALO, :] = jnp.where(seq_start, 0.0, hh).astype(h_ref.dtype)
        h_ref[HALO:, :] = _rms(x_ref[...], g_ref[...]).astype(h_ref.dtype)
        o_ref[...] = x_ref[...]

    h = h_ref[...]

    def conv(z, c_ref):
        z1 = pltpu.roll(z, 1, axis=0)
        z2 = pltpu.roll(z, 2, axis=0)
        y = c_ref[2:3, :] * z + c_ref[1:2, :] * z1 + c_ref[0:1, :] * z2
        return y[HALO:, :]

    gate = conv(jnp.dot(h, wg_ref[...], preferred_element_type=jnp.float32), cg_ref)
    up = conv(jnp.dot(h, wu_ref[...], preferred_element_type=jnp.float32), cu_ref)
    a = (gate * jax.nn.sigmoid(gate) * up).astype(jnp.bfloat16)
    o_ref[...] += jnp.dot(a, wd_ref[...], preferred_element_type=jnp.float32)


def conv_ffn(x, g, wg, wu, cg, cu, wd, *, seq, tm, tf):
    m, d = x.shape
    hb = tm // HALO
    return pl.pallas_call(
        functools.partial(_ffn_kernel, seq // tm),
        out_shape=jax.ShapeDtypeStruct((m, d), jnp.float32),
        grid=(m // tm, wg.shape[1] // tf),
        in_specs=[
            pl.BlockSpec((tm, d), lambda i, f: (i, 0), pipeline_mode=pl.Buffered(1)),
            pl.BlockSpec((HALO, d), lambda i, f: (jnp.maximum(i * hb - 1, 0), 0)),
            pl.BlockSpec((1, d), lambda i, f: (0, 0)),
            pl.BlockSpec((d, tf), lambda i, f: (0, f)),
            pl.BlockSpec((d, tf), lambda i, f: (0, f)),
            pl.BlockSpec((CONV_WIDTH, tf), lambda i, f: (0, f)),
            pl.BlockSpec((CONV_WIDTH, tf), lambda i, f: (0, f)),
            pl.BlockSpec((tf, d), lambda i, f: (f, 0)),
        ],
        out_specs=pl.BlockSpec((tm, d), lambda i, f: (i, 0)),
        scratch_shapes=[pltpu.VMEM((tm + HALO, d), jnp.bfloat16)],
        compiler_params=_cp(("parallel", "arbitrary")),
        name="conv_ffn",
    )(x, x, g.reshape(1, d), wg, wu, cg, cu, wd)


def _ple_kernel(x_ref, g_ref, p_ref, wg_ref, wp_ref, o_ref, h_ref):
    j = pl.program_id(1)
    tn = o_ref.shape[1]

    @pl.when(j == 0)
    def _():
        h_ref[...] = _rms(x_ref[...], g_ref[...]).astype(h_ref.dtype)

    gate = jax.nn.sigmoid(jnp.dot(h_ref[...], wg_ref[...], preferred_element_type=jnp.float32))
    pp = jnp.dot(p_ref[...].astype(jnp.bfloat16), wp_ref[...], preferred_element_type=jnp.float32)
    o_ref[...] = x_ref[:, pl.ds(pl.multiple_of(j * tn, LANE), tn)] + gate * pp


def ple(x, g, p, wg, wp, *, tm, tn):
    m, d = x.shape
    pd = p.shape[1]
    return pl.pallas_call(
        _ple_kernel,
        out_shape=jax.ShapeDtypeStruct((m, d), jnp.float32),
        grid=(m // tm, d // tn),
        in_specs=[
            pl.BlockSpec((tm, d), lambda i, j: (i, 0)),
            pl.BlockSpec((1, d), lambda i, j: (0, 0)),
            pl.BlockSpec((tm, pd), lambda i, j: (i, 0)),
            pl.BlockSpec((d, tn), lambda i, j: (0, j)),
            pl.BlockSpec((pd, tn), lambda i, j: (0, j)),
        ],
        out_specs=pl.BlockSpec((tm, tn), lambda i, j: (i, j)),
        scratch_shapes=[pltpu.VMEM((tm, d), jnp.bfloat16)],
        compiler_params=_cp(("parallel", "arbitrary")),
        name="ple",
    )(x, g.reshape(1, d), p, wg, wp)


def _dsa_prep_kernel(cq_ref, ckv_ref, gcq_ref, gckv_ref, gq_ref, wuq_ref, wqi_ref, q_ref, qi_ref, kv_ref, kvt_ref):
    tm = cq_ref.shape[0]
    cq = _rms(cq_ref[...], gcq_ref[...]).astype(jnp.bfloat16)
    kv = _rms(ckv_ref[...], gckv_ref[...])
    kv_ref[...] = kv.astype(kv_ref.dtype)
    kvt_ref[0:KV_LORA, :] = kv.T.astype(kvt_ref.dtype)
    kvt_ref[KV_LORA:, :] = jnp.ones((KVT_ROWS - KV_LORA, tm), kvt_ref.dtype)
    qi_ref[...] = jnp.dot(cq, wqi_ref[...], preferred_element_type=jnp.float32).astype(qi_ref.dtype)
    att_scale = KV_LORA ** -0.5
    for h in range(H_A):
        ql = jnp.dot(cq, wuq_ref[:, h * KV_LORA:(h + 1) * KV_LORA], preferred_element_type=jnp.float32)
        qn = (_rms(ql, gq_ref[...]) * att_scale).astype(q_ref.dtype)
        for r in range(tm // BLOCK):
            q_ref[(r * H_A + h) * BLOCK:(r * H_A + h + 1) * BLOCK, :] = qn[r * BLOCK:(r + 1) * BLOCK, :]


def dsa_prep(proj, gcq, gckv, gq, wuq, wqi, *, seq, tm):
    m = proj.shape[0]
    cq_blk = 1024 // Q_LORA
    ckv_blk = 1536 // KV_LORA
    nt = seq // tm
    return pl.pallas_call(
        _dsa_prep_kernel,
        out_shape=(
            jax.ShapeDtypeStruct((m * H_A, KV_LORA), jnp.bfloat16),
            jax.ShapeDtypeStruct((m, IDX_HEADS * IDX_DIM), jnp.bfloat16),
            jax.ShapeDtypeStruct((m, KV_LORA), jnp.bfloat16),
            jax.ShapeDtypeStruct((m // seq, KVT_ROWS, seq), jnp.bfloat16),
        ),
        grid=(m // tm,),
        in_specs=[
            pl.BlockSpec((tm, Q_LORA), lambda i: (i, cq_blk)),
            pl.BlockSpec((tm, KV_LORA), lambda i: (i, ckv_blk)),
            pl.BlockSpec((1, Q_LORA), lambda i: (0, 0)),
            pl.BlockSpec((1, KV_LORA), lambda i: (0, 0)),
            pl.BlockSpec((1, KV_LORA), lambda i: (0, 0)),
            pl.BlockSpec((Q_LORA, H_A * KV_LORA), lambda i: (0, 0)),
            pl.BlockSpec((Q_LORA, IDX_HEADS * IDX_DIM), lambda i: (0, 0)),
        ],
        out_specs=(
            pl.BlockSpec((tm * H_A, KV_LORA), lambda i: (i, 0)),
            pl.BlockSpec((tm, IDX_HEADS * IDX_DIM), lambda i: (i, 0)),
            pl.BlockSpec((tm, KV_LORA), lambda i: (i, 0)),
            pl.BlockSpec((None, KVT_ROWS, tm), lambda i: (i // nt, 0, i % nt)),
        ),
        compiler_params=_cp(("parallel",)),
        name="dsa_prep",
    )(proj, proj, gcq.reshape(1, -1), gckv.reshape(1, -1), gq.reshape(1, -1), wuq, wqi)


def _dsa_attn_kernel(topk, g_off, q_ref, qi_ref, kwq_ref, kwall_ref, kv_ref, kvt_ref, g_ref, wuv_ref, o_ref,
                     ka_ref, kb_ref, key_ref, s_ref, acc_ref):
    n = pl.program_id(1)
    per_tile = KEY_TILE // BLOCK
    n_tiles = (n + per_tile) // per_tile
    q_pos = n * BLOCK + lax.broadcasted_iota(jnp.int32, (KEY_TILE, BLOCK), 1)
    k_row = lax.broadcasted_iota(jnp.int32, (KEY_TILE, BLOCK), 0)

    @pl.when(n == 0)
    def _():
        kw = kwall_ref[...]
        lo = lax.broadcasted_iota(jnp.int32, kw.shape, 1) < IDX_DIM
        ka_ref[...] = jnp.where(lo, kw, 0.0).astype(ka_ref.dtype)
        kb_ref[...] = jnp.where(lo, 0.0, pltpu.roll(kw, IDX_DIM, axis=1)).astype(kb_ref.dtype)

    w_t = kwq_ref[...].T

    def score_tile(j, carry):
        col = pl.multiple_of(j * KEY_TILE, KEY_TILE)
        ka = ka_ref[pl.ds(col, KEY_TILE), :]
        kb = kb_ref[pl.ds(col, KEY_TILE), :]
        s = jnp.zeros((KEY_TILE, BLOCK), jnp.float32)
        for p in range(IDX_HEADS // 2):
            qp = qi_ref[:, p * LANE:(p + 1) * LANE]
            da = lax.dot_general(ka, qp, _NT, preferred_element_type=jnp.float32)
            db = lax.dot_general(kb, qp, _NT, preferred_element_type=jnp.float32)
            s = s + w_t[IDX_DIM + 2 * p:IDX_DIM + 2 * p + 1, :] * jnp.maximum(da, 0.0)
            s = s + w_t[IDX_DIM + 2 * p + 1:IDX_DIM + 2 * p + 2, :] * jnp.maximum(db, 0.0)
        s = jnp.where(col + k_row <= q_pos, s, -jnp.inf)
        bits = pltpu.bitcast(s, jnp.int32)
        key_ref[pl.ds(col, KEY_TILE), :] = jnp.where(bits < 0, bits ^ jnp.int32(0x7FFFFFFF), bits)
        return carry

    lax.fori_loop(0, n_tiles, score_tile, 0)

    def bit_step(b, tu):
        cand_u = tu | lax.shift_left(jnp.int32(1), jnp.int32(31) - b)
        cand_s = cand_u ^ jnp.int32(INT_MIN)

        def count_tile(j, cnt):
            col = pl.multiple_of(j * KEY_TILE, KEY_TILE)
            hit = jnp.where(key_ref[pl.ds(col, KEY_TILE), :] >= cand_s, 1, 0)
            return cnt + jnp.sum(hit.reshape(KEY_TILE // 8, 8, BLOCK), axis=0)

        cnt = lax.fori_loop(0, n_tiles, count_tile, jnp.zeros((8, BLOCK), jnp.int32))
        total = jnp.sum(cnt, axis=0, keepdims=True)
        return jnp.where(total >= topk, cand_u, tu)

    thr = lax.fori_loop(0, 32, bit_step, jnp.zeros((1, BLOCK), jnp.int32)) ^ jnp.int32(INT_MIN)

    group_w = (H_A // DSA_GROUPS) * BLOCK
    for grp in range(DSA_GROUPS):
        cs = slice(grp * group_w, (grp + 1) * group_w)
        q_grp = q_ref[cs, :]

        def logits_tile(j, m):
            col = pl.multiple_of(j * KEY_TILE, KEY_TILE)
            sel = (key_ref[pl.ds(col, KEY_TILE), :] >= thr) & (col + k_row <= q_pos)
            am = jnp.where(sel, 0.0, NEG)
            s = lax.dot_general(kv_ref[pl.ds(col, KEY_TILE), :], q_grp, _NT, preferred_element_type=jnp.float32)
            grow = pl.multiple_of(g_off - n * BLOCK + col, LANE)
            s = s + (g_ref[pl.ds(grow, KEY_TILE), cs] + jnp.concatenate([am] * (H_A // DSA_GROUPS), axis=1))
            s_ref[pl.ds(col, KEY_TILE), :] = s
            return jnp.maximum(m, jnp.max(s, axis=0, keepdims=True))

        m = lax.fori_loop(0, n_tiles, logits_tile, jnp.full((1, group_w), NEG, jnp.float32))
        acc_ref[...] = jnp.zeros_like(acc_ref)

        def value_tile(j, carry):
            col = pl.multiple_of(j * KEY_TILE, KEY_TILE)
            p = jnp.exp(s_ref[pl.ds(col, KEY_TILE), :] - m).astype(jnp.bfloat16)
            acc_ref[...] += jnp.dot(kvt_ref[:, pl.ds(col, KEY_TILE)], p, preferred_element_type=jnp.float32)
            return carry

        lax.fori_loop(0, n_tiles, value_tile, 0)

        o_t = (acc_ref[0:KV_LORA, :] / acc_ref[KV_LORA:KV_LORA + 1, :]).astype(jnp.bfloat16)
        for pp in range(H_A // DSA_GROUPS // 2):
            p = grp * (H_A // DSA_GROUPS // 2) + pp
            pair = jnp.concatenate([o_t[:, (2 * pp) * BLOCK:(2 * pp + 1) * BLOCK],
                                    o_t[:, (2 * pp + 1) * BLOCK:(2 * pp + 2) * BLOCK]], axis=0)
            y_t = jnp.dot(wuv_ref[p], pair, preferred_element_type=jnp.float32)
            o_ref[:, p * LANE:(p + 1) * LANE] = y_t.T.astype(o_ref.dtype)


def dsa_attn(q, qi, proj, kv, kvt, gtab, wuv2, *, batch, seq):
    nb = seq // BLOCK
    topk = min(TOPK_MAX, seq // TOPK_DIV)
    kw_blk = 1792 // LANE
    g_off = seq - BLOCK
    return pl.pallas_call(
        functools.partial(_dsa_attn_kernel, topk, g_off),
        out_shape=jax.ShapeDtypeStruct((batch * seq, H_A * HEAD_DIM), jnp.bfloat16),
        grid=(batch, nb),
        in_specs=[
            pl.BlockSpec((H_A * BLOCK, KV_LORA), lambda b, n: (b * nb + n, 0)),
            pl.BlockSpec((BLOCK, IDX_HEADS * IDX_DIM), lambda b, n: (b * nb + n, 0)),
            pl.BlockSpec((BLOCK, LANE), lambda b, n: (b * nb + n, kw_blk)),
            pl.BlockSpec((seq, LANE), lambda b, n: (b, kw_blk)),
            pl.BlockSpec((seq, KV_LORA), lambda b, n: (b, 0)),
            pl.BlockSpec((None, KVT_ROWS, seq), lambda b, n: (b, 0, 0)),
            pl.BlockSpec(gtab.shape, lambda b, n: (0, 0), pipeline_mode=pl.Buffered(1)),
            pl.BlockSpec(wuv2.shape, lambda b, n: (0, 0, 0)),
        ],
        out_specs=pl.BlockSpec((BLOCK, H_A * HEAD_DIM), lambda b, n: (b * nb + n, 0)),
        scratch_shapes=[
            pltpu.VMEM((seq, LANE), jnp.bfloat16),
            pltpu.VMEM((seq, LANE), jnp.bfloat16),
            pltpu.VMEM((seq, BLOCK), jnp.int32),
            pltpu.VMEM((seq, H_A // DSA_GROUPS * BLOCK), jnp.float32),
            pltpu.VMEM((KVT_ROWS, H_A // DSA_GROUPS * BLOCK), jnp.float32),
        ],
        compiler_params=_cp(("parallel", "arbitrary")),
        name="dsa_attn",
    )(q, qi, proj, proj, kv, kvt, gtab, wuv2)


def _swa_kernel(q_ref, kc_ref, kp_ref, vc_ref, vp_ref, gq_ref, gk_ref, bias_ref, sink_ref, o_ref):
    n = pl.program_id(1)
    g = H_B // KV_B
    scale = HEAD_DIM ** -0.5
    kn = _pair_norm(jnp.concatenate([kp_ref[...], kc_ref[...]], axis=0), gk_ref[...])
    k_nat = kn.astype(jnp.bfloat16)
    k_swap = pltpu.roll(kn, HEAD_DIM, axis=1).astype(jnp.bfloat16)
    v = jnp.concatenate([vp_ref[...], vc_ref[...]], axis=0).astype(jnp.bfloat16)
    col = lax.broadcasted_iota(jnp.int32, (BLOCK, 2 * BLOCK), 1)
    first = jnp.where((n == 0) & (col < BLOCK), NEG, 0.0)
    lo = lax.broadcasted_iota(jnp.int32, (BLOCK, LANE), 1) < HEAD_DIM
    for p in range(H_B // 2):
        hk = (2 * p) // g
        qn = _pair_norm(q_ref[:, p * LANE:(p + 1) * LANE], gq_ref[...]) * scale
        halves = []
        for half in range(2):
            head = 2 * p + half
            qh = (jnp.where(lo, qn, 0.0) if half == 0 else jnp.where(lo, 0.0, qn)).astype(jnp.bfloat16)
            kh = k_nat if half == hk else k_swap
            s = lax.dot_general(qh, kh, _NT, preferred_element_type=jnp.float32) + bias_ref[head] + first
            sink = sink_ref[head]
            mx = jnp.maximum(jnp.max(s, axis=-1, keepdims=True), sink)
            e = jnp.exp(s - mx)
            den = jnp.sum(e, axis=-1, keepdims=True) + jnp.exp(sink - mx)
            o = jnp.dot((e / den).astype(jnp.bfloat16), v, preferred_element_type=jnp.float32)
            halves.append(o if half == hk else pltpu.roll(o, HEAD_DIM, axis=1))
        o_ref[:, p * LANE:(p + 1) * LANE] = jnp.where(lo, halves[0], halves[1]).astype(o_ref.dtype)


def swa_attn(proj, gq, gk, bias_tab, sinks, *, batch, seq):
    nb = seq // BLOCK
    k_blk = 1920 // LANE
    v_blk = 2048 // LANE
    prev = lambda b, n: b * nb + jnp.maximum(n - 1, 0)
    return pl.pallas_call(
        _swa_kernel,
        out_shape=jax.ShapeDtypeStruct((batch * seq, H_B * HEAD_DIM), jnp.bfloat16),
        grid=(batch, nb),
        in_specs=[
            pl.BlockSpec((BLOCK, H_B * HEAD_DIM), lambda b, n: (b * nb + n, 0)),
            pl.BlockSpec((BLOCK, LANE), lambda b, n: (b * nb + n, k_blk)),
            pl.BlockSpec((BLOCK, LANE), lambda b, n: (prev(b, n), k_blk)),
            pl.BlockSpec((BLOCK, LANE), lambda b, n: (b * nb + n, v_blk)),
            pl.BlockSpec((BLOCK, LANE), lambda b, n: (prev(b, n), v_blk)),
            pl.BlockSpec((1, LANE), lambda b, n: (0, 0)),
            pl.BlockSpec((1, LANE), lambda b, n: (0, 0)),
            pl.BlockSpec(bias_tab.shape, lambda b, n: (0, 0, 0)),
            pl.BlockSpec(memory_space=pltpu.SMEM),
        ],
        out_specs=pl.BlockSpec((BLOCK, H_B * HEAD_DIM), lambda b, n: (b * nb + n, 0)),
        compiler_params=_cp(("parallel", "parallel")),
        name="swa_attn",
    )(proj, proj, proj, proj, proj, jnp.tile(gq, 2).reshape(1, LANE), jnp.tile(gk, 2).reshape(1, LANE),
      bias_tab, sinks)


def _top16(v):
    return pltpu.bitcast(pltpu.bitcast(v, jnp.int32) & jnp.int32(-65536), jnp.float32)


def _split3(c):
    hi = _top16(c)
    r = c - hi
    mid = _top16(r)
    return hi, mid, r - mid


def _head_inv_rms(x_ref, e_ref):
    ss = jnp.zeros((x_ref.shape[0], LANE), jnp.float32)
    for p in range(x_ref.shape[1] // LANE):
        x = x_ref[:, p * LANE:(p + 1) * LANE]
        x2 = x * x
        hi = _top16(x2)
        e_p = e_ref[p * LANE:(p + 1) * LANE, :]
        ss = ss + jnp.dot(hi.astype(jnp.bfloat16), e_p, preferred_element_type=jnp.float32)
        ss = ss + jnp.dot((x2 - hi).astype(jnp.bfloat16), e_p, preferred_element_type=jnp.float32)
    return jnp.concatenate([t.astype(jnp.bfloat16) for t in _split3(lax.rsqrt(ss / HEAD_DIM + EPS))], axis=1)


def _fox_prep_kernel(q_ref, k_ref, v_ref, f_ref, gq_ref, gk_ref, fb_ref, e_ref, et_ref, qo_ref, ko_ref, vt_ref,
                     carry_ref):
    tm = q_ref.shape[0]

    @pl.when(pl.program_id(1) == 0)
    def _():
        carry_ref[...] = jnp.zeros_like(carry_ref)

    z = f_ref[...] + fb_ref[...]
    logf = jnp.minimum(z, 0.0) - jnp.log1p(jnp.exp(-jnp.abs(z)))
    r = lax.broadcasted_iota(jnp.int32, (tm, tm), 0)
    c = lax.broadcasted_iota(jnp.int32, (tm, tm), 1)
    tri = jnp.where(c <= r, 1.0, 0.0)
    cum = jnp.dot(tri, logf, preferred_element_type=jnp.float32, precision=lax.Precision.HIGHEST) + carry_ref[...]
    carry_ref[...] = cum[tm - 1:tm, :]

    inv_q = _head_inv_rms(q_ref, e_ref)
    inv_k = _head_inv_rms(k_ref, e_ref)

    lane = lax.broadcasted_iota(jnp.int32, (tm, LANE), 1)
    masks = []
    for half in range(2):
        al = lane - (HEAD_DIM if half == 0 else 0)
        first, second = (al >= 0) & (al < 3), (al >= 3) & (al < 6)
        masks.append(((al == 0) | (al == 3), (al == 1) | (al == 4), first, second,
                      jnp.where(second, 1.0, 0.0), jnp.where(first, 1.0, 0.0),
                      (lane < HEAD_DIM) if half == 0 else (lane >= HEAD_DIM)))
    for p in range(H_C // 2):
        sl = slice(p * LANE, (p + 1) * LANE)
        qn = q_ref[:, sl] * jnp.dot(inv_q, et_ref[:, sl], preferred_element_type=jnp.float32) * gq_ref[:, sl]
        kn = k_ref[:, sl] * jnp.dot(inv_k, et_ref[:, sl], preferred_element_type=jnp.float32) * gk_ref[:, sl]
        for half in range(2):
            h = 2 * p + half
            is_hi, is_mid, first, second, ones_q, ones_k, own = masks[half]
            hi, mid, lo = _split3(jnp.broadcast_to(cum[:, h:h + 1], (tm, LANE)))
            part = jnp.where(is_hi, hi, jnp.where(is_mid, mid, lo))
            qo_ref[:, h * LANE:(h + 1) * LANE] = jnp.where(
                own, qn, jnp.where(first, part, ones_q)).astype(qo_ref.dtype)
            ko_ref[:, h * LANE:(h + 1) * LANE] = jnp.where(
                own, kn, jnp.where(second, -part, ones_k)).astype(ko_ref.dtype)
    ones = jnp.ones((V_ROWS - HEAD_DIM, tm), vt_ref.dtype)
    for p in range(H_C // 2):
        vt = v_ref[:, p * LANE:(p + 1) * LANE].T.astype(vt_ref.dtype)
        for half in range(2):
            h = 2 * p + half
            vt_ref[h * V_ROWS:h * V_ROWS + HEAD_DIM, :] = vt[half * HEAD_DIM:(half + 1) * HEAD_DIM, :]
            vt_ref[h * V_ROWS + HEAD_DIM:(h + 1) * V_ROWS, :] = ones


def fox_prep(proj, gq, gk, fbias, *, batch, seq, tm):
    m = proj.shape[0]
    hd = H_C * HEAD_DIM
    nt = seq // tm
    f_blk = 3 * hd // LANE
    fb = jnp.zeros((1, LANE), jnp.float32).at[0, :H_C].set(fbias)
    aug = jax.ShapeDtypeStruct((m, H_C * LANE), jnp.bfloat16)
    row = lambda b, i: b * nt + i
    e_np = (np.arange(hd)[:, None] // HEAD_DIM == np.arange(LANE)[None, :]).astype(np.float32)
    e = jnp.asarray(e_np, jnp.bfloat16)
    et = jnp.asarray(np.tile(e_np.T, (3, 1)), jnp.bfloat16)
    return pl.pallas_call(
        _fox_prep_kernel,
        out_shape=(aug, aug, jax.ShapeDtypeStruct((batch, H_C * V_ROWS, seq), jnp.bfloat16)),
        grid=(batch, nt),
        in_specs=[
            pl.BlockSpec((tm, hd), lambda b, i: (row(b, i), 0)),
            pl.BlockSpec((tm, hd), lambda b, i: (row(b, i), 1)),
            pl.BlockSpec((tm, hd), lambda b, i: (row(b, i), 2)),
            pl.BlockSpec((tm, LANE), lambda b, i: (row(b, i), f_blk)),
            pl.BlockSpec((1, hd), lambda b, i: (0, 0)),
            pl.BlockSpec((1, hd), lambda b, i: (0, 0)),
            pl.BlockSpec((1, LANE), lambda b, i: (0, 0)),
            pl.BlockSpec((hd, LANE), lambda b, i: (0, 0)),
            pl.BlockSpec((3 * LANE, hd), lambda b, i: (0, 0)),
        ],
        out_specs=(
            pl.BlockSpec((tm, H_C * LANE), lambda b, i: (row(b, i), 0)),
            pl.BlockSpec((tm, H_C * LANE), lambda b, i: (row(b, i), 0)),
            pl.BlockSpec((None, H_C * V_ROWS, tm), lambda b, i: (b, 0, i)),
        ),
        scratch_shapes=[pltpu.VMEM((1, LANE), jnp.float32)],
        compiler_params=_cp(("parallel", "arbitrary")),
        name="fox_prep",
    )(proj, proj, proj, proj, jnp.tile(gq * HEAD_DIM ** -0.5, H_C).reshape(1, hd), jnp.tile(gk, H_C).reshape(1, hd),
      fb, e, et)


def _fox_attn_kernel(tq, q_ref, k_ref, vt_ref, o_ref, s_ref, p_ref):
    seq = q_ref.shape[0]
    nq = seq // tq
    diag_ok = (lax.broadcasted_iota(jnp.int32, (tq, tq), 0) <= lax.broadcasted_iota(jnp.int32, (tq, tq), 1))

    def logits(i):
        nk = (i + 1) * tq
        for hh in range(2):
            hl = slice(hh * LANE, (hh + 1) * LANE)
            s = lax.dot_general(k_ref[0:nk, hl], q_ref[i * tq:nk, hl], _NT,
                                preferred_element_type=jnp.float32)
            if i > 0:
                s_ref[i % 2, hh, 0:i * tq, :] = s[:i * tq]
            s_ref[i % 2, hh, i * tq:nk, :] = jnp.where(diag_ok, s[i * tq:], NEG)

    def probs(i):
        nk = (i + 1) * tq
        for hh in range(2):
            s = s_ref[i % 2, hh, 0:nk, :]
            p_ref[i % 2, hh, 0:nk, :] = jnp.exp(s - jnp.max(s, axis=0, keepdims=True)).astype(p_ref.dtype)

    def values(i):
        nk = (i + 1) * tq
        outs = []
        for hh in range(2):
            acc = jnp.dot(vt_ref[hh * V_ROWS:(hh + 1) * V_ROWS, 0:nk], p_ref[i % 2, hh, 0:nk, :],
                          preferred_element_type=jnp.float32)
            outs.append(acc[:HEAD_DIM] / acc[HEAD_DIM:HEAD_DIM + 1])
        o_ref[i * tq:nk, :] = jnp.concatenate(outs, axis=0).T.astype(o_ref.dtype)

    logits(0)
    for i in range(nq):
        if i > 0:
            values(i - 1)
        if i + 1 < nq:
            logits(i + 1)
        probs(i)
    values(nq - 1)


def fox_attn(qa, ka, vt, *, batch, seq, tq):
    pairs = H_C // 2
    return pl.pallas_call(
        functools.partial(_fox_attn_kernel, tq),
        out_shape=jax.ShapeDtypeStruct((batch * seq, H_C * HEAD_DIM), jnp.bfloat16),
        grid=(batch, pairs),
        in_specs=[
            pl.BlockSpec((seq, 2 * LANE), lambda b, c: (b, c)),
            pl.BlockSpec((seq, 2 * LANE), lambda b, c: (b, c)),
            pl.BlockSpec((None, 2 * V_ROWS, seq), lambda b, c: (b, c, 0)),
        ],
        out_specs=pl.BlockSpec((seq, 2 * HEAD_DIM), lambda b, c: (b, c)),
        scratch_shapes=[pltpu.VMEM((2, 2, seq, tq), jnp.float32), pltpu.VMEM((2, 2, seq, tq), jnp.bfloat16)],
        compiler_params=_cp(("parallel", "parallel")),
        name="fox_attn",
    )(qa, ka, vt)


def _rel_bucket(dist):
    n = jnp.maximum(dist, 0)
    exact = N_BUCKETS // 2
    nf = jnp.maximum(n, 1).astype(jnp.float32)
    large = exact + (jnp.log(nf / exact) / math.log(MAX_DISTANCE / exact) * (N_BUCKETS - exact)).astype(jnp.int32)
    large = jnp.minimum(large, N_BUCKETS - 1)
    return jnp.where(n < exact, n, large)


def _pad_cols(w, n):
    return jnp.pad(w, ((0, 0), (0, n - w.shape[1])))


def _toeplitz(v, rows, cols, off):
    h, dmax = v.shape
    lt = cols + rows - 1
    idx = off + rows - 1 - np.arange(lt)
    e = jnp.where(((idx >= 0) & (idx < dmax))[None], v[:, np.clip(idx, 0, dmax - 1)], 0.0)
    f = jnp.pad(e, ((0, 0), (0, 1)))
    a = jnp.tile(f, (1, rows))[:, :rows * lt].reshape(h, rows, lt)
    return a[:, :, rows - 1:rows - 1 + cols]


def kernel(x, p, attn_norm, ffn_norm, ple_norm, rel_bias, w_in_even, a_cq_norm, a_ckv_norm, a_w_uq, a_q_norm,
           a_w_qidx, a_w_uv, b_q_norm, b_k_norm, b_sinks, w_out_even, w_in_odd, c_forget_bias, c_q_norm,
           c_k_norm, w_out_odd, w_up, ffn_conv, w_down, w_ple_gate, w_ple_proj):
    batch, seq, d = x.shape
    depth = p.shape[0]
    m = batch * seq
    bf = jnp.bfloat16
    d_ff = w_down.shape[1]
    f_pad = -(-d_ff // 512) * 512
    hd_c = H_C * HEAD_DIM

    bias_by_dist = rel_bias[_rel_bucket(jnp.arange(seq))].T
    g_off = seq - BLOCK
    g_cols = g_off + KEY_TILE
    gtab = jnp.transpose(_toeplitz(bias_by_dist[:H_A], BLOCK, g_cols, g_off), (2, 0, 1)).reshape(g_cols, H_A * BLOCK)
    dist_b = np.arange(BLOCK)[:, None] + BLOCK - np.arange(2 * BLOCK)[None, :]
    band = (dist_b >= 0) & (dist_b < WINDOW)
    btab = jnp.where(band[None], _toeplitz(bias_by_dist[H_A:, :WINDOW], BLOCK, 2 * BLOCK, BLOCK), NEG)

    o1 = Q_LORA
    o2 = o1 + KV_LORA
    o3 = o2 + IDX_DIM
    o4 = o3 + IDX_HEADS
    o5 = o4 + H_B * HEAD_DIM
    o6 = o5 + KV_B * HEAD_DIM

    x = x.reshape(m, d)
    for i in range(depth):
        if i % 2 == 0:
            e = i // 2
            w = w_in_even[e]
            w_in = jnp.concatenate(
                [w[:, o4:o5], w[:, :o1], w[:, o1:o2], _pad_cols(w[:, o2:o4], LANE), w[:, o5:o6], w[:, o6:],
                 jnp.zeros((d, LANE), w.dtype)], axis=1).astype(bf)
            proj = norm_matmul(x, attn_norm[i], w_in, tm=1024, tn=w_in.shape[1] // 3)
            q, qi, kv, kvt = dsa_prep(proj, a_cq_norm[e], a_ckv_norm[e], a_q_norm[e], a_w_uq[e].astype(bf),
                                      a_w_qidx[e].astype(bf), seq=seq, tm=256)
            wuv = jnp.swapaxes(a_w_uv[e], 1, 2).reshape(H_A // 2, 2, HEAD_DIM, KV_LORA)
            zero = jnp.zeros_like(wuv[:, 0])
            wuv2 = jnp.concatenate([jnp.concatenate([wuv[:, 0], zero], axis=2),
                                    jnp.concatenate([zero, wuv[:, 1]], axis=2)], axis=1).astype(bf)
            y_a = dsa_attn(q, qi, proj, kv, kvt, gtab, wuv2, batch=batch, seq=seq)
            y_b = swa_attn(proj, b_q_norm[e], b_k_norm[e], btab, b_sinks[e], batch=batch, seq=seq)
            wo = w_out_even[e].astype(bf)
            x = matmul_res([(y_a, wo[:H_A * HEAD_DIM]), (y_b, wo[H_A * HEAD_DIM:])], x, tm=1024, tn=512)
        else:
            o = i // 2
            w_in = jnp.pad(w_in_odd[o], ((0, 0), (0, 3 * hd_c + LANE - w_in_odd[o].shape[1]))).astype(bf)
            proj = norm_matmul(x, attn_norm[i], w_in, tm=1024, tn=w_in.shape[1] // 7)
            qa, ka, vt = fox_prep(proj, c_q_norm[o], c_k_norm[o], c_forget_bias[o], batch=batch, seq=seq, tm=256)
            y = fox_attn(qa, ka, vt, batch=batch, seq=seq, tq=256)
            x = matmul_res([(y, w_out_odd[o].astype(bf))], x, tm=1024, tn=512)

        wg = _pad_cols(w_up[i][:, :d_ff], f_pad).astype(bf)
        wu = _pad_cols(w_up[i][:, d_ff:], f_pad).astype(bf)
        cg = _pad_cols(ffn_conv[i][:, :d_ff], f_pad)
        cu = _pad_cols(ffn_conv[i][:, d_ff:], f_pad)
        wd = jnp.pad(w_down[i], ((0, f_pad - d_ff), (0, 0))).astype(bf)
        x = conv_ffn(x, ffn_norm[i], wg, wu, cg, cu, wd, seq=seq, tm=1024, tf=512)
        x = ple(x, ple_norm[i], p[i].reshape(m, -1), w_ple_gate[i].astype(bf), w_ple_proj[i].astype(bf),
                tm=1024, tn=512)
    return x.reshape(batch, seq, d)
```

```python
import functools
import math

import jax
import jax.numpy as jnp
import numpy as np
from jax import lax
from jax.experimental import pallas as pl
from jax.experimental.pallas import tpu as pltpu

HEAD_DIM = 64
BLOCK = 128
EPS = 1e-6
H_A = 16
Q_LORA = 512
KV_LORA = 256
IDX_HEADS = 16
IDX_DIM = 64
TOPK_MAX = 256
TOPK_DIV = 4
H_B = 16
KV_B = 2
WINDOW = 128
H_C = 32
N_BUCKETS = 32
MAX_DISTANCE = 1024
CONV_WIDTH = 3

LANE = 128
NEG = -1e30
KEY_TILE = 512
HALO = 16
DSA_GROUPS = 2
KVT_ROWS = KV_LORA + 16
V_ROWS = HEAD_DIM + 16
VMEM_LIMIT = 56 * 1024 * 1024
INT_MIN = -2 ** 31

_NT = (((1,), (1,)), ((), ()))


def _cp(sem, vmem=VMEM_LIMIT):
    return pltpu.CompilerParams(dimension_semantics=sem, vmem_limit_bytes=vmem)


def _rms(x, g):
    return x * lax.rsqrt(jnp.mean(x * x, axis=-1, keepdims=True) + EPS) * g


def _pair_norm(x, g2):
    lo = lax.broadcasted_iota(jnp.int32, x.shape, 1) < HEAD_DIM
    x2 = x * x
    s_lo = jnp.sum(jnp.where(lo, x2, 0.0), axis=-1, keepdims=True)
    s_hi = jnp.sum(jnp.where(lo, 0.0, x2), axis=-1, keepdims=True)
    inv = jnp.where(lo, lax.rsqrt(s_lo / HEAD_DIM + EPS), lax.rsqrt(s_hi / HEAD_DIM + EPS))
    return x * inv * g2


def _norm_matmul_kernel(x_ref, g_ref, w_ref, o_ref, h_ref):
    @pl.when(pl.program_id(1) == 0)
    def _():
        h_ref[...] = _rms(x_ref[...], g_ref[...]).astype(h_ref.dtype)

    o_ref[...] = jnp.dot(h_ref[...], w_ref[...], preferred_element_type=jnp.float32)


def norm_matmul(x, g, w, layer, *, tm, tn):
    m, d = x.shape
    n = w.shape[2]
    return pl.pallas_call(
        _norm_matmul_kernel,
        out_shape=jax.ShapeDtypeStruct((m, n), jnp.float32),
        grid=(m // tm, n // tn),
        in_specs=[
            pl.BlockSpec((tm, d), lambda i, j: (i, 0)),
            pl.BlockSpec((1, d), lambda i, j: (0, 0)),
            pl.BlockSpec((None, d, tn), lambda i, j: (layer, 0, j)),
        ],
        out_specs=pl.BlockSpec((tm, tn), lambda i, j: (i, j)),
        scratch_shapes=[pltpu.VMEM((tm, d), jnp.bfloat16)],
        compiler_params=_cp(("parallel", "arbitrary")),
        name="norm_matmul",
    )(x, g.reshape(1, d), w)


def _matmul_res_kernel(n_pairs, *refs):
    x_ref = refs[2 * n_pairs]
    o_ref = refs[2 * n_pairs + 1]
    acc = x_ref[...]
    for p in range(n_pairs):
        acc = acc + jnp.dot(refs[2 * p][...], refs[2 * p + 1][...], preferred_element_type=jnp.float32)
    o_ref[...] = acc


def matmul_res(acts, w, layer, x, *, tm, tn):
    m, n = x.shape
    in_specs, args = [], []
    for r, a in enumerate(acts):
        k = a.shape[1]
        in_specs += [pl.BlockSpec((tm, k), lambda i, j: (i, 0)),
                     pl.BlockSpec((None, k, tn), lambda i, j, r=r: (layer, r, j))]
        args += [a, w]
    in_specs.append(pl.BlockSpec((tm, tn), lambda i, j: (i, j)))
    return pl.pallas_call(
        functools.partial(_matmul_res_kernel, len(acts)),
        out_shape=jax.ShapeDtypeStruct((m, n), jnp.float32),
        grid=(m // tm, n // tn),
        in_specs=in_specs,
        out_specs=pl.BlockSpec((tm, tn), lambda i, j: (i, j)),
        compiler_params=_cp(("parallel", "parallel")),
        name="matmul_res",
    )(*args, x)


def _ffn_kernel(tiles_per_seq, x_ref, xh_ref, g_ref, wg_ref, wu_ref, cg_ref, cu_ref, wd_ref, o_ref, h_ref):
    i = pl.program_id(0)
    f = pl.program_id(1)

    @pl.when(f == 0)
    def _():
        seq_start = (i % tiles_per_seq) == 0
        hh = _rms(xh_ref[...], g_ref[...])
        h_ref[0:HALO, :] = jnp.where(seq_start, 0.0, hh).astype(h_ref.dtype)
        h_ref[HALO:, :] = _rms(x_ref[...], g_ref[...]).astype(h_ref.dtype)
        o_ref[...] = x_ref[...]

    h = h_ref[...]

    def conv(z, c_ref):
        z1 = pltpu.roll(z, 1, axis=0)
        z2 = pltpu.roll(z, 2, axis=0)
        y = c_ref[2:3, :] * z + c_ref[1:2, :] * z1 + c_ref[0:1, :] * z2
        return y[HALO:, :]

    gate = conv(jnp.dot(h, wg_ref[...], preferred_element_type=jnp.float32), cg_ref)
    up = conv(jnp.dot(h, wu_ref[...], preferred_element_type=jnp.float32), cu_ref)
    a = (gate * jax.nn.sigmoid(gate) * up).astype(jnp.bfloat16)
    o_ref[...] += jnp.dot(a, wd_ref[...], preferred_element_type=jnp.float32)


def conv_ffn(x, g, wgu, cgu, wd, layer, *, seq, tm, tf):
    m, d = x.shape
    hb = tm // HALO
    return pl.pallas_call(
        functools.partial(_ffn_kernel, seq // tm),
        out_shape=jax.ShapeDtypeStruct((m, d), jnp.float32),
        grid=(m // tm, wgu.shape[3] // tf),
        in_specs=[
            pl.BlockSpec((tm, d), lambda i, f: (i, 0), pipeline_mode=pl.Buffered(1)),
            pl.BlockSpec((HALO, d), lambda i, f: (jnp.maximum(i * hb - 1, 0), 0)),
            pl.BlockSpec((1, d), lambda i, f: (0, 0)),
            pl.BlockSpec((None, None, d, tf), lambda i, f: (layer, 0, 0, f)),
            pl.BlockSpec((None, None, d, tf), lambda i, f: (layer, 1, 0, f)),
            pl.BlockSpec((None, None, CONV_WIDTH, tf), lambda i, f: (layer, 0, 0, f)),
            pl.BlockSpec((None, None, CONV_WIDTH, tf), lambda i, f: (layer, 1, 0, f)),
            pl.BlockSpec((None, tf, d), lambda i, f: (layer, f, 0)),
        ],
        out_specs=pl.BlockSpec((tm, d), lambda i, f: (i, 0)),
        scratch_shapes=[pltpu.VMEM((tm + HALO, d), jnp.bfloat16)],
        compiler_params=_cp(("parallel", "arbitrary")),
        name="conv_ffn",
    )(x, x, g.reshape(1, d), wgu, wgu, cgu, cgu, wd)


def _ple_kernel(x_ref, g_ref, p_ref, wg_ref, wp_ref, o_ref, h_ref):
    j = pl.program_id(1)
    tn = o_ref.shape[1]

    @pl.when(j == 0)
    def _():
        h_ref[...] = _rms(x_ref[...], g_ref[...]).astype(h_ref.dtype)

    gate = jax.nn.sigmoid(jnp.dot(h_ref[...], wg_ref[...], preferred_element_type=jnp.float32))
    pp = jnp.dot(p_ref[...].astype(jnp.bfloat16), wp_ref[...], preferred_element_type=jnp.float32)
    o_ref[...] = x_ref[:, pl.ds(pl.multiple_of(j * tn, LANE), tn)] + gate * pp


def ple(x, g, p, wg, wp, layer, *, tm, tn):
    m, d = x.shape
    pd = p.shape[2]
    return pl.pallas_call(
        _ple_kernel,
        out_shape=jax.ShapeDtypeStruct((m, d), jnp.float32),
        grid=(m // tm, d // tn),
        in_specs=[
            pl.BlockSpec((tm, d), lambda i, j: (i, 0)),
            pl.BlockSpec((1, d), lambda i, j: (0, 0)),
            pl.BlockSpec((None, tm, pd), lambda i, j: (layer, i, 0)),
            pl.BlockSpec((None, d, tn), lambda i, j: (layer, 0, j)),
            pl.BlockSpec((None, pd, tn), lambda i, j: (layer, 0, j)),
        ],
        out_specs=pl.BlockSpec((tm, tn), lambda i, j: (i, j)),
        scratch_shapes=[pltpu.VMEM((tm, d), jnp.bfloat16)],
        compiler_params=_cp(("parallel", "arbitrary")),
        name="ple",
    )(x, g.reshape(1, d), p, wg, wp)


def _dsa_prep_kernel(cq_ref, ckv_ref, gcq_ref, gckv_ref, gq_ref, wuq_ref, wqi_ref, q_ref, qi_ref, kv_ref, kvt_ref):
    tm = cq_ref.shape[0]
    cq = _rms(cq_ref[...], gcq_ref[...]).astype(jnp.bfloat16)
    kv = _rms(ckv_ref[...], gckv_ref[...])
    kv_ref[...] = kv.astype(kv_ref.dtype)
    kvt_ref[0:KV_LORA, :] = kv.T.astype(kvt_ref.dtype)
    kvt_ref[KV_LORA:, :] = jnp.ones((KVT_ROWS - KV_LORA, tm), kvt_ref.dtype)
    qi_ref[...] = jnp.dot(cq, wqi_ref[...], preferred_element_type=jnp.float32).astype(qi_ref.dtype)
    att_scale = KV_LORA ** -0.5
    for h in range(H_A):
        ql = jnp.dot(cq, wuq_ref[:, h * KV_LORA:(h + 1) * KV_LORA], preferred_element_type=jnp.float32)
        qn = (_rms(ql, gq_ref[...]) * att_scale).astype(q_ref.dtype)
        for r in range(tm // BLOCK):
            q_ref[(r * H_A + h) * BLOCK:(r * H_A + h + 1) * BLOCK, :] = qn[r * BLOCK:(r + 1) * BLOCK, :]


def dsa_prep(proj, gcq, gckv, gq, wuq, wqi, layer, *, seq, tm):
    m = proj.shape[0]
    cq_blk = 1024 // Q_LORA
    ckv_blk = 1536 // KV_LORA
    nt = seq // tm
    return pl.pallas_call(
        _dsa_prep_kernel,
        out_shape=(
            jax.ShapeDtypeStruct((m * H_A, KV_LORA), jnp.bfloat16),
            jax.ShapeDtypeStruct((m, IDX_HEADS * IDX_DIM), jnp.bfloat16),
            jax.ShapeDtypeStruct((m, KV_LORA), jnp.bfloat16),
            jax.ShapeDtypeStruct((m // seq, KVT_ROWS, seq), jnp.bfloat16),
        ),
        grid=(m // tm,),
        in_specs=[
            pl.BlockSpec((tm, Q_LORA), lambda i: (i, cq_blk)),
            pl.BlockSpec((tm, KV_LORA), lambda i: (i, ckv_blk)),
            pl.BlockSpec((1, Q_LORA), lambda i: (0, 0)),
            pl.BlockSpec((1, KV_LORA), lambda i: (0, 0)),
            pl.BlockSpec((1, KV_LORA), lambda i: (0, 0)),
            pl.BlockSpec((None, Q_LORA, H_A * KV_LORA), lambda i: (layer, 0, 0)),
            pl.BlockSpec((None, Q_LORA, IDX_HEADS * IDX_DIM), lambda i: (layer, 0, 0)),
        ],
        out_specs=(
            pl.BlockSpec((tm * H_A, KV_LORA), lambda i: (i, 0)),
            pl.BlockSpec((tm, IDX_HEADS * IDX_DIM), lambda i: (i, 0)),
            pl.BlockSpec((tm, KV_LORA), lambda i: (i, 0)),
            pl.BlockSpec((None, KVT_ROWS, tm), lambda i: (i // nt, 0, i % nt)),
        ),
        compiler_params=_cp(("parallel",)),
        name="dsa_prep",
    )(proj, proj, gcq.reshape(1, -1), gckv.reshape(1, -1), gq.reshape(1, -1), wuq, wqi)


def _dsa_attn_kernel(topk, g_off, q_ref, qi_ref, kwq_ref, kwall_ref, kv_ref, kvt_ref, g_ref, wuv_ref, o_ref,
                     ka_ref, kb_ref, key_ref, s_ref, acc_ref):
    n = pl.program_id(1)
    per_tile = KEY_TILE // BLOCK
    n_tiles = (n + per_tile) // per_tile
    q_pos = n * BLOCK + lax.broadcasted_iota(jnp.int32, (KEY_TILE, BLOCK), 1)
    k_row = lax.broadcasted_iota(jnp.int32, (KEY_TILE, BLOCK), 0)

    @pl.when(n == 0)
    def _():
        kw = kwall_ref[...]
        lo = lax.broadcasted_iota(jnp.int32, kw.shape, 1) < IDX_DIM
        ka_ref[...] = jnp.where(lo, kw, 0.0).astype(ka_ref.dtype)
        kb_ref[...] = jnp.where(lo, 0.0, pltpu.roll(kw, IDX_DIM, axis=1)).astype(kb_ref.dtype)

    w_t = kwq_ref[...].T

    def score_tile(j, carry):
        col = pl.multiple_of(j * KEY_TILE, KEY_TILE)
        ka = ka_ref[pl.ds(col, KEY_TILE), :]
        kb = kb_ref[pl.ds(col, KEY_TILE), :]
        s = jnp.zeros((KEY_TILE, BLOCK), jnp.float32)
        for p in range(IDX_HEADS // 2):
            qp = qi_ref[:, p * LANE:(p + 1) * LANE]
            da = lax.dot_general(ka, qp, _NT, preferred_element_type=jnp.float32)
            db = lax.dot_general(kb, qp, _NT, preferred_element_type=jnp.float32)
            s = s + w_t[IDX_DIM + 2 * p:IDX_DIM + 2 * p + 1, :] * jnp.maximum(da, 0.0)
            s = s + w_t[IDX_DIM + 2 * p + 1:IDX_DIM + 2 * p + 2, :] * jnp.maximum(db, 0.0)
        s = jnp.where(col + k_row <= q_pos, s, -jnp.inf)
        bits = pltpu.bitcast(s, jnp.int32)
        key_ref[pl.ds(col, KEY_TILE), :] = jnp.where(bits < 0, bits ^ jnp.int32(0x7FFFFFFF), bits)
        return carry

    lax.fori_loop(0, n_tiles, score_tile, 0)

    def bit_step(b, tu):
        cand_u = tu | lax.shift_left(jnp.int32(1), jnp.int32(31) - b)
        cand_s = cand_u ^ jnp.int32(INT_MIN)

        def count_tile(j, cnt):
            col = pl.multiple_of(j * KEY_TILE, KEY_TILE)
            hit = jnp.where(key_ref[pl.ds(col, KEY_TILE), :] >= cand_s, 1, 0)
            return cnt + jnp.sum(hit.reshape(KEY_TILE // 8, 8, BLOCK), axis=0)

        cnt = lax.fori_loop(0, n_tiles, count_tile, jnp.zeros((8, BLOCK), jnp.int32))
        total = jnp.sum(cnt, axis=0, keepdims=True)
        return jnp.where(total >= topk, cand_u, tu)

    thr = lax.fori_loop(0, 32, bit_step, jnp.zeros((1, BLOCK), jnp.int32)) ^ jnp.int32(INT_MIN)

    group_w = (H_A // DSA_GROUPS) * BLOCK
    for grp in range(DSA_GROUPS):
        cs = slice(grp * group_w, (grp + 1) * group_w)
        q_grp = q_ref[cs, :]

        def logits_tile(j, m):
            col = pl.multiple_of(j * KEY_TILE, KEY_TILE)
            sel = (key_ref[pl.ds(col, KEY_TILE), :] >= thr) & (col + k_row <= q_pos)
            am = jnp.where(sel, 0.0, NEG)
            s = lax.dot_general(kv_ref[pl.ds(col, KEY_TILE), :], q_grp, _NT, preferred_element_type=jnp.float32)
            grow = pl.multiple_of(g_off - n * BLOCK + col, LANE)
            s = s + (g_ref[pl.ds(grow, KEY_TILE), cs] + jnp.concatenate([am] * (H_A // DSA_GROUPS), axis=1))
            s_ref[pl.ds(col, KEY_TILE), :] = s
            return jnp.maximum(m, jnp.max(s, axis=0, keepdims=True))

        m = lax.fori_loop(0, n_tiles, logits_tile, jnp.full((1, group_w), NEG, jnp.float32))
        acc_ref[...] = jnp.zeros_like(acc_ref)

        def value_tile(j, carry):
            col = pl.multiple_of(j * KEY_TILE, KEY_TILE)
            p = jnp.exp(s_ref[pl.ds(col, KEY_TILE), :] - m).astype(jnp.bfloat16)
            acc_ref[...] += jnp.dot(kvt_ref[:, pl.ds(col, KEY_TILE)], p, preferred_element_type=jnp.float32)
            return carry

        lax.fori_loop(0, n_tiles, value_tile, 0)

        o_t = (acc_ref[0:KV_LORA, :] / acc_ref[KV_LORA:KV_LORA + 1, :]).astype(jnp.bfloat16)
        for pp in range(H_A // DSA_GROUPS // 2):
            p = grp * (H_A // DSA_GROUPS // 2) + pp
            pair = jnp.concatenate([o_t[:, (2 * pp) * BLOCK:(2 * pp + 1) * BLOCK],
                                    o_t[:, (2 * pp + 1) * BLOCK:(2 * pp + 2) * BLOCK]], axis=0)
            y_t = jnp.dot(wuv_ref[p], pair, preferred_element_type=jnp.float32)
            o_ref[:, p * LANE:(p + 1) * LANE] = y_t.T.astype(o_ref.dtype)


def dsa_attn(q, qi, proj, kv, kvt, gtab, wuv2, *, batch, seq):
    nb = seq // BLOCK
    topk = min(TOPK_MAX, seq // TOPK_DIV)
    kw_blk = 1792 // LANE
    g_off = seq - BLOCK
    return pl.pallas_call(
        functools.partial(_dsa_attn_kernel, topk, g_off),
        out_shape=jax.ShapeDtypeStruct((batch * seq, H_A * HEAD_DIM), jnp.bfloat16),
        grid=(batch, nb),
        in_specs=[
            pl.BlockSpec((H_A * BLOCK, KV_LORA), lambda b, n: (b * nb + n, 0)),
            pl.BlockSpec((BLOCK, IDX_HEADS * IDX_DIM), lambda b, n: (b * nb + n, 0)),
            pl.BlockSpec((BLOCK, LANE), lambda b, n: (b * nb + n, kw_blk)),
            pl.BlockSpec((seq, LANE), lambda b, n: (b, kw_blk)),
            pl.BlockSpec((seq, KV_LORA), lambda b, n: (b, 0)),
            pl.BlockSpec((None, KVT_ROWS, seq), lambda b, n: (b, 0, 0)),
            pl.BlockSpec(gtab.shape, lambda b, n: (0, 0), pipeline_mode=pl.Buffered(1)),
            pl.BlockSpec(wuv2.shape, lambda b, n: (0, 0, 0)),
        ],
        out_specs=pl.BlockSpec((BLOCK, H_A * HEAD_DIM), lambda b, n: (b * nb + n, 0)),
        scratch_shapes=[
            pltpu.VMEM((seq, LANE), jnp.bfloat16),
            pltpu.VMEM((seq, LANE), jnp.bfloat16),
            pltpu.VMEM((seq, BLOCK), jnp.int32),
            pltpu.VMEM((seq, H_A // DSA_GROUPS * BLOCK), jnp.float32),
            pltpu.VMEM((KVT_ROWS, H_A // DSA_GROUPS * BLOCK), jnp.float32),
        ],
        compiler_params=_cp(("parallel", "arbitrary")),
        name="dsa_attn",
    )(q, qi, proj, proj, kv, kvt, gtab, wuv2)


def _swa_kernel(q_ref, kc_ref, kp_ref, vc_ref, vp_ref, gq_ref, gk_ref, bias_ref, sink_ref, o_ref):
    n = pl.program_id(1)
    g = H_B // KV_B
    scale = HEAD_DIM ** -0.5
    kn = _pair_norm(jnp.concatenate([kp_ref[...], kc_ref[...]], axis=0), gk_ref[...])
    k_nat = kn.astype(jnp.bfloat16)
    k_swap = pltpu.roll(kn, HEAD_DIM, axis=1).astype(jnp.bfloat16)
    v = jnp.concatenate([vp_ref[...], vc_ref[...]], axis=0).astype(jnp.bfloat16)
    col = lax.broadcasted_iota(jnp.int32, (BLOCK, 2 * BLOCK), 1)
    first = jnp.where((n == 0) & (col < BLOCK), NEG, 0.0)
    lo = lax.broadcasted_iota(jnp.int32, (BLOCK, LANE), 1) < HEAD_DIM
    for p in range(H_B // 2):
        hk = (2 * p) // g
        qn = _pair_norm(q_ref[:, p * LANE:(p + 1) * LANE], gq_ref[...]) * scale
        halves = []
        for half in range(2):
            head = 2 * p + half
            qh = (jnp.where(lo, qn, 0.0) if half == 0 else jnp.where(lo, 0.0, qn)).astype(jnp.bfloat16)
            kh = k_nat if half == hk else k_swap
            s = lax.dot_general(qh, kh, _NT, preferred_element_type=jnp.float32) + bias_ref[head] + first
            sink = sink_ref[head]
            mx = jnp.maximum(jnp.max(s, axis=-1, keepdims=True), sink)
            e = jnp.exp(s - mx)
            den = jnp.sum(e, axis=-1, keepdims=True) + jnp.exp(sink - mx)
            o = jnp.dot((e / den).astype(jnp.bfloat16), v, preferred_element_type=jnp.float32)
            halves.append(o if half == hk else pltpu.roll(o, HEAD_DIM, axis=1))
        o_ref[:, p * LANE:(p + 1) * LANE] = jnp.where(lo, halves[0], halves[1]).astype(o_ref.dtype)


def swa_attn(proj, gq, gk, bias_tab, sinks, *, batch, seq):
    nb = seq // BLOCK
    k_blk = 1920 // LANE
    v_blk = 2048 // LANE
    prev = lambda b, n: b * nb + jnp.maximum(n - 1, 0)
    return pl.pallas_call(
        _swa_kernel,
        out_shape=jax.ShapeDtypeStruct((batch * seq, H_B * HEAD_DIM), jnp.bfloat16),
        grid=(batch, nb),
        in_specs=[
            pl.BlockSpec((BLOCK, H_B * HEAD_DIM), lambda b, n: (b * nb + n, 0)),
            pl.BlockSpec((BLOCK, LANE), lambda b, n: (b * nb + n, k_blk)),
            pl.BlockSpec((BLOCK, LANE), lambda b, n: (prev(b, n), k_blk)),
            pl.BlockSpec((BLOCK, LANE), lambda b, n: (b * nb + n, v_blk)),
            pl.BlockSpec((BLOCK, LANE), lambda b, n: (prev(b, n), v_blk)),
            pl.BlockSpec((1, LANE), lambda b, n: (0, 0)),
            pl.BlockSpec((1, LANE), lambda b, n: (0, 0)),
            pl.BlockSpec(bias_tab.shape, lambda b, n: (0, 0, 0)),
            pl.BlockSpec(memory_space=pltpu.SMEM),
        ],
        out_specs=pl.BlockSpec((BLOCK, H_B * HEAD_DIM), lambda b, n: (b * nb + n, 0)),
        compiler_params=_cp(("parallel", "parallel")),
        name="swa_attn",
    )(proj, proj, proj, proj, proj, jnp.tile(gq, 2).reshape(1, LANE), jnp.tile(gk, 2).reshape(1, LANE),
      bias_tab, sinks)


def _top16(v):
    return pltpu.bitcast(pltpu.bitcast(v, jnp.int32) & jnp.int32(-65536), jnp.float32)


def _split3(c):
    hi = _top16(c)
    r = c - hi
    mid = _top16(r)
    return hi, mid, r - mid


def _head_inv_rms(x_ref, e_ref):
    ss = jnp.zeros((x_ref.shape[0], LANE), jnp.float32)
    for p in range(x_ref.shape[1] // LANE):
        x = x_ref[:, p * LANE:(p + 1) * LANE]
        x2 = x * x
        hi = _top16(x2)
        e_p = e_ref[p * LANE:(p + 1) * LANE, :]
        ss = ss + jnp.dot(hi.astype(jnp.bfloat16), e_p, preferred_element_type=jnp.float32)
        ss = ss + jnp.dot((x2 - hi).astype(jnp.bfloat16), e_p, preferred_element_type=jnp.float32)
    return jnp.concatenate([t.astype(jnp.bfloat16) for t in _split3(lax.rsqrt(ss / HEAD_DIM + EPS))], axis=1)


def _fox_prep_kernel(q_ref, k_ref, v_ref, f_ref, gq_ref, gk_ref, fb_ref, e_ref, et_ref, qo_ref, ko_ref, vt_ref,
                     carry_ref):
    tm = q_ref.shape[0]

    @pl.when(pl.program_id(1) == 0)
    def _():
        carry_ref[...] = jnp.zeros_like(carry_ref)

    z = f_ref[...] + fb_ref[...]
    logf = jnp.minimum(z, 0.0) - jnp.log1p(jnp.exp(-jnp.abs(z)))
    r = lax.broadcasted_iota(jnp.int32, (tm, tm), 0)
    c = lax.broadcasted_iota(jnp.int32, (tm, tm), 1)
    tri = jnp.where(c <= r, 1.0, 0.0)
    cum = jnp.dot(tri, logf, preferred_element_type=jnp.float32, precision=lax.Precision.HIGHEST) + carry_ref[...]
    carry_ref[...] = cum[tm - 1:tm, :]

    inv_q = _head_inv_rms(q_ref, e_ref)
    inv_k = _head_inv_rms(k_ref, e_ref)

    lane = lax.broadcasted_iota(jnp.int32, (tm, LANE), 1)
    masks = []
    for half in range(2):
        al = lane - (HEAD_DIM if half == 0 else 0)
        first, second = (al >= 0) & (al < 3), (al >= 3) & (al < 6)
        masks.append(((al == 0) | (al == 3), (al == 1) | (al == 4), first, second,
                      jnp.where(second, 1.0, 0.0), jnp.where(first, 1.0, 0.0),
                      (lane < HEAD_DIM) if half == 0 else (lane >= HEAD_DIM)))
    for p in range(H_C // 2):
        sl = slice(p * LANE, (p + 1) * LANE)
        qn = q_ref[:, sl] * jnp.dot(inv_q, et_ref[:, sl], preferred_element_type=jnp.float32) * gq_ref[:, sl]
        kn = k_ref[:, sl] * jnp.dot(inv_k, et_ref[:, sl], preferred_element_type=jnp.float32) * gk_ref[:, sl]
        for half in range(2):
            h = 2 * p + half
            is_hi, is_mid, first, second, ones_q, ones_k, own = masks[half]
            hi, mid, lo = _split3(jnp.broadcast_to(cum[:, h:h + 1], (tm, LANE)))
            part = jnp.where(is_hi, hi, jnp.where(is_mid, mid, lo))
            qo_ref[:, h * LANE:(h + 1) * LANE] = jnp.where(
                own, qn, jnp.where(first, part, ones_q)).astype(qo_ref.dtype)
            ko_ref[:, h * LANE:(h + 1) * LANE] = jnp.where(
                own, kn, jnp.where(second, -part, ones_k)).astype(ko_ref.dtype)
    ones = jnp.ones((V_ROWS - HEAD_DIM, tm), vt_ref.dtype)
    for p in range(H_C // 2):
        vt = v_ref[:, p * LANE:(p + 1) * LANE].T.astype(vt_ref.dtype)
        for half in range(2):
            h = 2 * p + half
            vt_ref[h * V_ROWS:h * V_ROWS + HEAD_DIM, :] = vt[half * HEAD_DIM:(half + 1) * HEAD_DIM, :]
            vt_ref[h * V_ROWS + HEAD_DIM:(h + 1) * V_ROWS, :] = ones


def fox_prep(proj, gq, gk, fbias, *, batch, seq, tm):
    m = proj.shape[0]
    hd = H_C * HEAD_DIM
    nt = seq // tm
    f_blk = 3 * hd // LANE
    fb = jnp.zeros((1, LANE), jnp.float32).at[0, :H_C].set(fbias)
    aug = jax.ShapeDtypeStruct((m, H_C * LANE), jnp.bfloat16)
    row = lambda b, i: b * nt + i
    e_np = (np.arange(hd)[:, None] // HEAD_DIM == np.arange(LANE)[None, :]).astype(np.float32)
    e = jnp.asarray(e_np, jnp.bfloat16)
    et = jnp.asarray(np.tile(e_np.T, (3, 1)), jnp.bfloat16)
    return pl.pallas_call(
        _fox_prep_kernel,
        out_shape=(aug, aug, jax.ShapeDtypeStruct((batch, H_C * V_ROWS, seq), jnp.bfloat16)),
        grid=(batch, nt),
        in_specs=[
            pl.BlockSpec((tm, hd), lambda b, i: (row(b, i), 0)),
            pl.BlockSpec((tm, hd), lambda b, i: (row(b, i), 1)),
            pl.BlockSpec((tm, hd), lambda b, i: (row(b, i), 2)),
            pl.BlockSpec((tm, LANE), lambda b, i: (row(b, i), f_blk)),
            pl.BlockSpec((1, hd), lambda b, i: (0, 0)),
            pl.BlockSpec((1, hd), lambda b, i: (0, 0)),
            pl.BlockSpec((1, LANE), lambda b, i: (0, 0)),
            pl.BlockSpec((hd, LANE), lambda b, i: (0, 0)),
            pl.BlockSpec((3 * LANE, hd), lambda b, i: (0, 0)),
        ],
        out_specs=(
            pl.BlockSpec((tm, H_C * LANE), lambda b, i: (row(b, i), 0)),
            pl.BlockSpec((tm, H_C * LANE), lambda b, i: (row(b, i), 0)),
            pl.BlockSpec((None, H_C * V_ROWS, tm), lambda b, i: (b, 0, i)),
        ),
        scratch_shapes=[pltpu.VMEM((1, LANE), jnp.float32)],
        compiler_params=_cp(("parallel", "arbitrary")),
        name="fox_prep",
    )(proj, proj, proj, proj, jnp.tile(gq * HEAD_DIM ** -0.5, H_C).reshape(1, hd), jnp.tile(gk, H_C).reshape(1, hd),
      fb, e, et)


def _fox_attn_kernel(tq, q_ref, k_ref, vt_ref, o_ref, s_ref, p_ref):
    seq = q_ref.shape[0]
    nq = seq // tq
    diag_ok = (lax.broadcasted_iota(jnp.int32, (tq, tq), 0) <= lax.broadcasted_iota(jnp.int32, (tq, tq), 1))

    def logits(i):
        nk = (i + 1) * tq
        for hh in range(2):
            hl = slice(hh * LANE, (hh + 1) * LANE)
            s = lax.dot_general(k_ref[0:nk, hl], q_ref[i * tq:nk, hl], _NT,
                                preferred_element_type=jnp.float32)
            if i > 0:
                s_ref[i % 2, hh, 0:i * tq, :] = s[:i * tq]
            s_ref[i % 2, hh, i * tq:nk, :] = jnp.where(diag_ok, s[i * tq:], NEG)

    def probs(i):
        nk = (i + 1) * tq
        for hh in range(2):
            s = s_ref[i % 2, hh, 0:nk, :]
            p_ref[i % 2, hh, 0:nk, :] = jnp.exp(s - jnp.max(s, axis=0, keepdims=True)).astype(p_ref.dtype)

    def values(i):
        nk = (i + 1) * tq
        outs = []
        for hh in range(2):
            acc = jnp.dot(vt_ref[hh * V_ROWS:(hh + 1) * V_ROWS, 0:nk], p_ref[i % 2, hh, 0:nk, :],
                          preferred_element_type=jnp.float32)
            outs.append(acc[:HEAD_DIM] / acc[HEAD_DIM:HEAD_DIM + 1])
        o_ref[i * tq:nk, :] = jnp.concatenate(outs, axis=0).T.astype(o_ref.dtype)

    logits(0)
    for i in range(nq):
        if i > 0:
            values(i - 1)
        if i + 1 < nq:
            logits(i + 1)
        probs(i)
    values(nq - 1)


def fox_attn(qa, ka, vt, *, batch, seq, tq):
    pairs = H_C // 2
    return pl.pallas_call(
        functools.partial(_fox_attn_kernel, tq),
        out_shape=jax.ShapeDtypeStruct((batch * seq, H_C * HEAD_DIM), jnp.bfloat16),
        grid=(batch, pairs),
        in_specs=[
            pl.BlockSpec((seq, 2 * LANE), lambda b, c: (b, c)),
            pl.BlockSpec((seq, 2 * LANE), lambda b, c: (b, c)),
            pl.BlockSpec((None, 2 * V_ROWS, seq), lambda b, c: (b, c, 0)),
        ],
        out_specs=pl.BlockSpec((seq, 2 * HEAD_DIM), lambda b, c: (b, c)),
        scratch_shapes=[pltpu.VMEM((2, 2, seq, tq), jnp.float32), pltpu.VMEM((2, 2, seq, tq), jnp.bfloat16)],
        compiler_params=_cp(("parallel", "parallel")),
        name="fox_attn",
    )(qa, ka, vt)


def _rel_bucket(dist):
    n = jnp.maximum(dist, 0)
    exact = N_BUCKETS // 2
    nf = jnp.maximum(n, 1).astype(jnp.float32)
    large = exact + (jnp.log(nf / exact) / math.log(MAX_DISTANCE / exact) * (N_BUCKETS - exact)).astype(jnp.int32)
    large = jnp.minimum(large, N_BUCKETS - 1)
    return jnp.where(n < exact, n, large)


def _pad_cols(w, n):
    return jnp.pad(w, ((0, 0), (0, n - w.shape[1])))


def _toeplitz(v, rows, cols, off):
    h, dmax = v.shape
    lt = cols + rows - 1
    idx = off + rows - 1 - np.arange(lt)
    e = jnp.where(((idx >= 0) & (idx < dmax))[None], v[:, np.clip(idx, 0, dmax - 1)], 0.0)
    f = jnp.pad(e, ((0, 0), (0, 1)))
    a = jnp.tile(f, (1, rows))[:, :rows * lt].reshape(h, rows, lt)
    return a[:, :, rows - 1:rows - 1 + cols]


def kernel(x, p, attn_norm, ffn_norm, ple_norm, rel_bias, w_in_even, a_cq_norm, a_ckv_norm, a_w_uq, a_q_norm,
           a_w_qidx, a_w_uv, b_q_norm, b_k_norm, b_sinks, w_out_even, w_in_odd, c_forget_bias, c_q_norm,
           c_k_norm, w_out_odd, w_up, ffn_conv, w_down, w_ple_gate, w_ple_proj):
    batch, seq, d = x.shape
    depth = p.shape[0]
    m = batch * seq
    bf = jnp.bfloat16
    d_ff = w_down.shape[1]
    f_pad = -(-d_ff // 512) * 512
    hd_c = H_C * HEAD_DIM

    bias_by_dist = rel_bias[_rel_bucket(jnp.arange(seq))].T
    g_off = seq - BLOCK
    g_cols = g_off + KEY_TILE
    gtab = jnp.transpose(_toeplitz(bias_by_dist[:H_A], BLOCK, g_cols, g_off), (2, 0, 1)).reshape(g_cols, H_A * BLOCK)
    dist_b = np.arange(BLOCK)[:, None] + BLOCK - np.arange(2 * BLOCK)[None, :]
    band = (dist_b >= 0) & (dist_b < WINDOW)
    btab = jnp.where(band[None], _toeplitz(bias_by_dist[H_A:, :WINDOW], BLOCK, 2 * BLOCK, BLOCK), NEG)

    o1 = Q_LORA
    o2 = o1 + KV_LORA
    o3 = o2 + IDX_DIM
    o4 = o3 + IDX_HEADS
    o5 = o4 + H_B * HEAD_DIM
    o6 = o5 + KV_B * HEAD_DIM

    w = w_in_even
    w_in_e = jnp.concatenate(
        [w[:, :, o4:o5], w[:, :, :o1], w[:, :, o1:o2], w[:, :, o2:o4],
         jnp.zeros(w.shape[:2] + (LANE - (o4 - o2),), w.dtype), w[:, :, o5:o6], w[:, :, o6:],
         jnp.zeros(w.shape[:2] + (LANE,), w.dtype)], axis=2).astype(bf)
    w_in_o = jnp.pad(w_in_odd, ((0, 0), (0, 0), (0, 3 * hd_c + LANE - w_in_odd.shape[2]))).astype(bf)
    w_uq = a_w_uq.astype(bf)
    w_qi = a_w_qidx.astype(bf)
    w_out_e = w_out_even.astype(bf)
    w_out_o = w_out_odd.astype(bf)
    pad_f = ((0, 0), (0, 0), (0, 0), (0, f_pad - d_ff))
    wgu = jnp.pad(jnp.swapaxes(w_up.reshape(-1, d, 2, d_ff), 1, 2), pad_f).astype(bf)
    cgu = jnp.pad(jnp.swapaxes(ffn_conv.reshape(-1, CONV_WIDTH, 2, d_ff), 1, 2), pad_f)
    wd = jnp.pad(w_down, ((0, 0), (0, f_pad - d_ff), (0, 0))).astype(bf)
    w_pg = w_ple_gate.astype(bf)
    w_pp = w_ple_proj.astype(bf)
    p = p.reshape(depth, m, -1)
    wuv = jnp.swapaxes(a_w_uv, 2, 3).reshape(-1, H_A // 2, 2, HEAD_DIM, KV_LORA)
    zero = jnp.zeros_like(wuv[:, :, 0])
    wuv2 = jnp.concatenate([jnp.concatenate([wuv[:, :, 0], zero], axis=3),
                            jnp.concatenate([zero, wuv[:, :, 1]], axis=3)], axis=2).astype(bf)

    x = x.reshape(m, d)
    for i in range(depth):
        if i % 2 == 0:
            e = i // 2
            proj = norm_matmul(x, attn_norm[i], w_in_e, e, tm=1024, tn=w_in_e.shape[2] // 3)
            q, qi, kv, kvt = dsa_prep(proj, a_cq_norm[e], a_ckv_norm[e], a_q_norm[e], w_uq, w_qi, e, seq=seq, tm=256)
            y_a = dsa_attn(q, qi, proj, kv, kvt, gtab, wuv2[e], batch=batch, seq=seq)
            y_b = swa_attn(proj, b_q_norm[e], b_k_norm[e], btab, b_sinks[e], batch=batch, seq=seq)
            x = matmul_res([y_a, y_b], w_out_e, e, x, tm=1024, tn=512)
        else:
            o = i // 2
            proj = norm_matmul(x, attn_norm[i], w_in_o, o, tm=1024, tn=w_in_o.shape[2] // 7)
            qa, ka, vt = fox_prep(proj, c_q_norm[o], c_k_norm[o], c_forget_bias[o], batch=batch, seq=seq, tm=256)
            y = fox_attn(qa, ka, vt, batch=batch, seq=seq, tq=256)
            x = matmul_res([y], w_out_o, o, x, tm=1024, tn=512)
        x = conv_ffn(x, ffn_norm[i], wgu, cgu, wd, i, seq=seq, tm=1024, tf=512)
        x = ple(x, ple_norm[i], p, w_pg, w_pp, i, tm=1024, tn=512)
    return x.reshape(batch, seq, d)
```

```python
import functools
import math

import jax
import jax.numpy as jnp
import numpy as np
from jax import lax
from jax.experimental import pallas as pl
from jax.experimental.pallas import tpu as pltpu

HEAD_DIM = 64
BLOCK = 128
EPS = 1e-6
H_A = 16
Q_LORA = 512
KV_LORA = 256
IDX_HEADS = 16
IDX_DIM = 64
TOPK_MAX = 256
TOPK_DIV = 4
H_B = 16
KV_B = 2
WINDOW = 128
H_C = 32
N_BUCKETS = 32
MAX_DISTANCE = 1024
CONV_WIDTH = 3

LANE = 128
NEG = -1e30
KEY_TILE = 512
HALO = 16
DSA_GROUPS = 2
KVT_ROWS = KV_LORA + 16
V_ROWS = HEAD_DIM + 16
VMEM_LIMIT = 56 * 1024 * 1024
INT_MIN = -2 ** 31

_NT = (((1,), (1,)), ((), ()))


def _cp(sem, vmem=VMEM_LIMIT):
    return pltpu.CompilerParams(dimension_semantics=sem, vmem_limit_bytes=vmem)


def _rms(x, g):
    return x * lax.rsqrt(jnp.mean(x * x, axis=-1, keepdims=True) + EPS) * g


def _pair_norm(x, g2):
    lo = lax.broadcasted_iota(jnp.int32, x.shape, 1) < HEAD_DIM
    x2 = x * x
    s_lo = jnp.sum(jnp.where(lo, x2, 0.0), axis=-1, keepdims=True)
    s_hi = jnp.sum(jnp.where(lo, 0.0, x2), axis=-1, keepdims=True)
    inv = jnp.where(lo, lax.rsqrt(s_lo / HEAD_DIM + EPS), lax.rsqrt(s_hi / HEAD_DIM + EPS))
    return x * inv * g2


def _cast_cols_kernel(segments, x_ref, o_ref):
    o_ref[...] = jnp.zeros_like(o_ref)
    for src, length, dst in segments:
        o_ref[:, dst:dst + length] = x_ref[:, src:src + length].astype(o_ref.dtype)


def cast_cols(w, segments, out_c, *, tr):
    l, r, c = w.shape
    return pl.pallas_call(
        functools.partial(_cast_cols_kernel, tuple(segments)),
        out_shape=jax.ShapeDtypeStruct((l, r, out_c), jnp.bfloat16),
        grid=(l, r // tr),
        in_specs=[pl.BlockSpec((None, tr, c), lambda a, b: (a, b, 0))],
        out_specs=pl.BlockSpec((None, tr, out_c), lambda a, b: (a, b, 0)),
        compiler_params=_cp(("parallel", "parallel")),
        name="cast_cols",
    )(w)


def _cast_rows_kernel(n_src, x_ref, o_ref):
    r = pl.program_id(1)

    @pl.when(r < n_src)
    def _():
        o_ref[...] = x_ref[...].astype(o_ref.dtype)

    @pl.when(r >= n_src)
    def _():
        o_ref[...] = jnp.zeros_like(o_ref)


def cast_rows(w, out_r, *, tr):
    l, r, c = w.shape
    n_src = r // tr
    return pl.pallas_call(
        functools.partial(_cast_rows_kernel, n_src),
        out_shape=jax.ShapeDtypeStruct((l, out_r, c), jnp.bfloat16),
        grid=(l, out_r // tr),
        in_specs=[pl.BlockSpec((None, tr, c), lambda a, b: (a, jnp.minimum(b, n_src - 1), 0))],
        out_specs=pl.BlockSpec((None, tr, c), lambda a, b: (a, b, 0)),
        compiler_params=_cp(("parallel", "parallel")),
        name="cast_rows",
    )(w)


def _norm_matmul_kernel(x_ref, g_ref, w_ref, o_ref, h_ref):
    @pl.when(pl.program_id(1) == 0)
    def _():
        h_ref[...] = _rms(x_ref[...], g_ref[...]).astype(h_ref.dtype)

    o_ref[...] = jnp.dot(h_ref[...], w_ref[...], preferred_element_type=jnp.float32)


def norm_matmul(x, g, w, layer, *, tm, tn):
    m, d = x.shape
    n = w.shape[2]
    return pl.pallas_call(
        _norm_matmul_kernel,
        out_shape=jax.ShapeDtypeStruct((m, n), jnp.float32),
        grid=(m // tm, n // tn),
        in_specs=[
            pl.BlockSpec((tm, d), lambda i, j: (i, 0)),
            pl.BlockSpec((1, d), lambda i, j: (0, 0)),
            pl.BlockSpec((None, d, tn), lambda i, j: (layer, 0, j)),
        ],
        out_specs=pl.BlockSpec((tm, tn), lambda i, j: (i, j)),
        scratch_shapes=[pltpu.VMEM((tm, d), jnp.bfloat16)],
        compiler_params=_cp(("parallel", "arbitrary")),
        name="norm_matmul",
    )(x, g.reshape(1, d), w)


def _matmul_res_kernel(n_pairs, *refs):
    x_ref = refs[2 * n_pairs]
    o_ref = refs[2 * n_pairs + 1]
    acc = x_ref[...]
    for p in range(n_pairs):
        acc = acc + jnp.dot(refs[2 * p][...], refs[2 * p + 1][...], preferred_element_type=jnp.float32)
    o_ref[...] = acc


def matmul_res(acts, w, layer, x, *, tm, tn):
    m, n = x.shape
    in_specs, args = [], []
    for r, a in enumerate(acts):
        k = a.shape[1]
        in_specs += [pl.BlockSpec((tm, k), lambda i, j: (i, 0)),
                     pl.BlockSpec((None, k, tn), lambda i, j, r=r: (layer, r, j))]
        args += [a, w]
    in_specs.append(pl.BlockSpec((tm, tn), lambda i, j: (i, j)))
    return pl.pallas_call(
        functools.partial(_matmul_res_kernel, len(acts)),
        out_shape=jax.ShapeDtypeStruct((m, n), jnp.float32),
        grid=(m // tm, n // tn),
        in_specs=in_specs,
        out_specs=pl.BlockSpec((tm, tn), lambda i, j: (i, j)),
        compiler_params=_cp(("parallel", "parallel")),
        name="matmul_res",
    )(*args, x)


def _ffn_kernel(tiles_per_seq, x_ref, xh_ref, g_ref, wg_ref, wu_ref, cg_ref, cu_ref, wd_ref, o_ref, h_ref):
    i = pl.program_id(0)
    f = pl.program_id(1)

    @pl.when(f == 0)
    def _():
        seq_start = (i % tiles_per_seq) == 0
        hh = _rms(xh_ref[...], g_ref[...])
        h_ref[0:HALO, :] = jnp.where(seq_start, 0.0, hh).astype(h_ref.dtype)
        h_ref[HALO:, :] = _rms(x_ref[...], g_ref[...]).astype(h_ref.dtype)
        o_ref[...] = x_ref[...]

    h = h_ref[...]

    def conv(z, c_ref):
        z1 = pltpu.roll(z, 1, axis=0)
        z2 = pltpu.roll(z, 2, axis=0)
        y = c_ref[2:3, :] * z + c_ref[1:2, :] * z1 + c_ref[0:1, :] * z2
        return y[HALO:, :]

    gate = conv(jnp.dot(h, wg_ref[...], preferred_element_type=jnp.float32), cg_ref)
    up = conv(jnp.dot(h, wu_ref[...], preferred_element_type=jnp.float32), cu_ref)
    a = (gate * jax.nn.sigmoid(gate) * up).astype(jnp.bfloat16)
    o_ref[...] += jnp.dot(a, wd_ref[...], preferred_element_type=jnp.float32)


def conv_ffn(x, g, wgu, cgu, wd, layer, *, seq, tm, tf):
    m, d = x.shape
    hb = tm // HALO
    nf = wd.shape[1] // tf
    return pl.pallas_call(
        functools.partial(_ffn_kernel, seq // tm),
        out_shape=jax.ShapeDtypeStruct((m, d), jnp.float32),
        grid=(m // tm, nf),
        in_specs=[
            pl.BlockSpec((tm, d), lambda i, f: (i, 0), pipeline_mode=pl.Buffered(1)),
            pl.BlockSpec((HALO, d), lambda i, f: (jnp.maximum(i * hb - 1, 0), 0)),
            pl.BlockSpec((1, d), lambda i, f: (0, 0)),
            pl.BlockSpec((None, d, tf), lambda i, f: (layer, 0, f)),
            pl.BlockSpec((None, d, tf), lambda i, f: (layer, 0, nf + f)),
            pl.BlockSpec((None, CONV_WIDTH, tf), lambda i, f: (layer, 0, f)),
            pl.BlockSpec((None, CONV_WIDTH, tf), lambda i, f: (layer, 0, nf + f)),
            pl.BlockSpec((None, tf, d), lambda i, f: (layer, f, 0)),
        ],
        out_specs=pl.BlockSpec((tm, d), lambda i, f: (i, 0)),
        scratch_shapes=[pltpu.VMEM((tm + HALO, d), jnp.bfloat16)],
        compiler_params=_cp(("parallel", "arbitrary")),
        name="conv_ffn",
    )(x, x, g.reshape(1, d), wgu, wgu, cgu, cgu, wd)


def _ple_kernel(x_ref, g_ref, p_ref, wg_ref, wp_ref, o_ref, h_ref):
    j = pl.program_id(1)
    tn = o_ref.shape[1]

    @pl.when(j == 0)
    def _():
        h_ref[...] = _rms(x_ref[...], g_ref[...]).astype(h_ref.dtype)

    gate = jax.nn.sigmoid(jnp.dot(h_ref[...], wg_ref[...], preferred_element_type=jnp.float32))
    pp = jnp.dot(p_ref[...].astype(jnp.bfloat16), wp_ref[...], preferred_element_type=jnp.float32)
    o_ref[...] = x_ref[:, pl.ds(pl.multiple_of(j * tn, LANE), tn)] + gate * pp


def ple(x, g, p, wg, wp, layer, *, tm, tn):
    m, d = x.shape
    pd = p.shape[2]
    return pl.pallas_call(
        _ple_kernel,
        out_shape=jax.ShapeDtypeStruct((m, d), jnp.float32),
        grid=(m // tm, d // tn),
        in_specs=[
            pl.BlockSpec((tm, d), lambda i, j: (i, 0)),
            pl.BlockSpec((1, d), lambda i, j: (0, 0)),
            pl.BlockSpec((None, tm, pd), lambda i, j: (layer, i, 0)),
            pl.BlockSpec((None, d, tn), lambda i, j: (layer, 0, j)),
            pl.BlockSpec((None, pd, tn), lambda i, j: (layer, 0, j)),
        ],
        out_specs=pl.BlockSpec((tm, tn), lambda i, j: (i, j)),
        scratch_shapes=[pltpu.VMEM((tm, d), jnp.bfloat16)],
        compiler_params=_cp(("parallel", "arbitrary")),
        name="ple",
    )(x, g.reshape(1, d), p, wg, wp)


def _dsa_prep_kernel(cq_ref, ckv_ref, gcq_ref, gckv_ref, gq_ref, wuq_ref, wqi_ref, q_ref, qi_ref, kv_ref, kvt_ref):
    tm = cq_ref.shape[0]
    cq = _rms(cq_ref[...], gcq_ref[...]).astype(jnp.bfloat16)
    kv = _rms(ckv_ref[...], gckv_ref[...])
    kv_ref[...] = kv.astype(kv_ref.dtype)
    kvt_ref[0:KV_LORA, :] = kv.T.astype(kvt_ref.dtype)
    kvt_ref[KV_LORA:, :] = jnp.ones((KVT_ROWS - KV_LORA, tm), kvt_ref.dtype)
    qi_ref[...] = jnp.dot(cq, wqi_ref[...], preferred_element_type=jnp.float32).astype(qi_ref.dtype)
    att_scale = KV_LORA ** -0.5
    for h in range(H_A):
        ql = jnp.dot(cq, wuq_ref[:, h * KV_LORA:(h + 1) * KV_LORA], preferred_element_type=jnp.float32)
        qn = (_rms(ql, gq_ref[...]) * att_scale).astype(q_ref.dtype)
        for r in range(tm // BLOCK):
            q_ref[(r * H_A + h) * BLOCK:(r * H_A + h + 1) * BLOCK, :] = qn[r * BLOCK:(r + 1) * BLOCK, :]


def dsa_prep(proj, gcq, gckv, gq, wuq, wqi, layer, *, seq, tm):
    m = proj.shape[0]
    cq_blk = 1024 // Q_LORA
    ckv_blk = 1536 // KV_LORA
    nt = seq // tm
    return pl.pallas_call(
        _dsa_prep_kernel,
        out_shape=(
            jax.ShapeDtypeStruct((m * H_A, KV_LORA), jnp.bfloat16),
            jax.ShapeDtypeStruct((m, IDX_HEADS * IDX_DIM), jnp.bfloat16),
            jax.ShapeDtypeStruct((m, KV_LORA), jnp.bfloat16),
            jax.ShapeDtypeStruct((m // seq, KVT_ROWS, seq), jnp.bfloat16),
        ),
        grid=(m // tm,),
        in_specs=[
            pl.BlockSpec((tm, Q_LORA), lambda i: (i, cq_blk)),
            pl.BlockSpec((tm, KV_LORA), lambda i: (i, ckv_blk)),
            pl.BlockSpec((1, Q_LORA), lambda i: (0, 0)),
            pl.BlockSpec((1, KV_LORA), lambda i: (0, 0)),
            pl.BlockSpec((1, KV_LORA), lambda i: (0, 0)),
            pl.BlockSpec((None, Q_LORA, H_A * KV_LORA), lambda i: (layer, 0, 0)),
            pl.BlockSpec((None, Q_LORA, IDX_HEADS * IDX_DIM), lambda i: (layer, 0, 0)),
        ],
        out_specs=(
            pl.BlockSpec((tm * H_A, KV_LORA), lambda i: (i, 0)),
            pl.BlockSpec((tm, IDX_HEADS * IDX_DIM), lambda i: (i, 0)),
            pl.BlockSpec((tm, KV_LORA), lambda i: (i, 0)),
            pl.BlockSpec((None, KVT_ROWS, tm), lambda i: (i // nt, 0, i % nt)),
        ),
        compiler_params=_cp(("parallel",)),
        name="dsa_prep",
    )(proj, proj, gcq.reshape(1, -1), gckv.reshape(1, -1), gq.reshape(1, -1), wuq, wqi)


def _dsa_attn_kernel(topk, g_off, q_ref, qi_ref, kwq_ref, kwall_ref, kv_ref, kvt_ref, g_ref, wuv_ref, o_ref,
                     ka_ref, kb_ref, key_ref, s_ref, acc_ref):
    n = pl.program_id(1)
    per_tile = KEY_TILE // BLOCK
    n_tiles = (n + per_tile) // per_tile
    q_pos = n * BLOCK + lax.broadcasted_iota(jnp.int32, (KEY_TILE, BLOCK), 1)
    k_row = lax.broadcasted_iota(jnp.int32, (KEY_TILE, BLOCK), 0)

    @pl.when(n == 0)
    def _():
        kw = kwall_ref[...]
        lo = lax.broadcasted_iota(jnp.int32, kw.shape, 1) < IDX_DIM
        ka_ref[...] = jnp.where(lo, kw, 0.0).astype(ka_ref.dtype)
        kb_ref[...] = jnp.where(lo, 0.0, pltpu.roll(kw, IDX_DIM, axis=1)).astype(kb_ref.dtype)

    w_t = kwq_ref[...].T

    def score_tile(j, carry):
        col = pl.multiple_of(j * KEY_TILE, KEY_TILE)
        ka = ka_ref[pl.ds(col, KEY_TILE), :]
        kb = kb_ref[pl.ds(col, KEY_TILE), :]
        s = jnp.zeros((KEY_TILE, BLOCK), jnp.float32)
        for p in range(IDX_HEADS // 2):
            qp = qi_ref[:, p * LANE:(p + 1) * LANE]
            da = lax.dot_general(ka, qp, _NT, preferred_element_type=jnp.float32)
            db = lax.dot_general(kb, qp, _NT, preferred_element_type=jnp.float32)
            s = s + w_t[IDX_DIM + 2 * p:IDX_DIM + 2 * p + 1, :] * jnp.maximum(da, 0.0)
            s = s + w_t[IDX_DIM + 2 * p + 1:IDX_DIM + 2 * p + 2, :] * jnp.maximum(db, 0.0)
        s = jnp.where(col + k_row <= q_pos, s, -jnp.inf)
        bits = pltpu.bitcast(s, jnp.int32)
        key_ref[pl.ds(col, KEY_TILE), :] = jnp.where(bits < 0, bits ^ jnp.int32(0x7FFFFFFF), bits)
        return carry

    lax.fori_loop(0, n_tiles, score_tile, 0)

    def bit_step(b, tu):
        cand_u = tu | lax.shift_left(jnp.int32(1), jnp.int32(31) - b)
        cand_s = cand_u ^ jnp.int32(INT_MIN)

        def count_tile(j, cnt):
            col = pl.multiple_of(j * KEY_TILE, KEY_TILE)
            hit = jnp.where(key_ref[pl.ds(col, KEY_TILE), :] >= cand_s, 1, 0)
            return cnt + jnp.sum(hit.reshape(KEY_TILE // 8, 8, BLOCK), axis=0)

        cnt = lax.fori_loop(0, n_tiles, count_tile, jnp.zeros((8, BLOCK), jnp.int32))
        total = jnp.sum(cnt, axis=0, keepdims=True)
        return jnp.where(total >= topk, cand_u, tu)

    thr = lax.fori_loop(0, 32, bit_step, jnp.zeros((1, BLOCK), jnp.int32)) ^ jnp.int32(INT_MIN)

    group_w = (H_A // DSA_GROUPS) * BLOCK
    for grp in range(DSA_GROUPS):
        cs = slice(grp * group_w, (grp + 1) * group_w)
        q_grp = q_ref[cs, :]

        def logits_tile(j, m):
            col = pl.multiple_of(j * KEY_TILE, KEY_TILE)
            sel = (key_ref[pl.ds(col, KEY_TILE), :] >= thr) & (col + k_row <= q_pos)
            am = jnp.where(sel, 0.0, NEG)
            s = lax.dot_general(kv_ref[pl.ds(col, KEY_TILE), :], q_grp, _NT, preferred_element_type=jnp.float32)
            grow = pl.multiple_of(g_off - n * BLOCK + col, LANE)
            s = s + (g_ref[pl.ds(grow, KEY_TILE), cs] + jnp.concatenate([am] * (H_A // DSA_GROUPS), axis=1))
            s_ref[pl.ds(col, KEY_TILE), :] = s
            return jnp.maximum(m, jnp.max(s, axis=0, keepdims=True))

        m = lax.fori_loop(0, n_tiles, logits_tile, jnp.full((1, group_w), NEG, jnp.float32))
        acc_ref[...] = jnp.zeros_like(acc_ref)

        def value_tile(j, carry):
            col = pl.multiple_of(j * KEY_TILE, KEY_TILE)
            p = jnp.exp(s_ref[pl.ds(col, KEY_TILE), :] - m).astype(jnp.bfloat16)
            acc_ref[...] += jnp.dot(kvt_ref[:, pl.ds(col, KEY_TILE)], p, preferred_element_type=jnp.float32)
            return carry

        lax.fori_loop(0, n_tiles, value_tile, 0)

        o_t = (acc_ref[0:KV_LORA, :] / acc_ref[KV_LORA:KV_LORA + 1, :]).astype(jnp.bfloat16)
        for pp in range(H_A // DSA_GROUPS // 2):
            p = grp * (H_A // DSA_GROUPS // 2) + pp
            pair = jnp.concatenate([o_t[:, (2 * pp) * BLOCK:(2 * pp + 1) * BLOCK],
                                    o_t[:, (2 * pp + 1) * BLOCK:(2 * pp + 2) * BLOCK]], axis=0)
            y_t = jnp.dot(wuv_ref[p], pair, preferred_element_type=jnp.float32)
            o_ref[:, p * LANE:(p + 1) * LANE] = y_t.T.astype(o_ref.dtype)


def dsa_attn(q, qi, proj, kv, kvt, gtab, wuv2, *, batch, seq):
    nb = seq // BLOCK
    topk = min(TOPK_MAX, seq // TOPK_DIV)
    kw_blk = 1792 // LANE
    g_off = seq - BLOCK
    return pl.pallas_call(
        functools.partial(_dsa_attn_kernel, topk, g_off),
        out_shape=jax.ShapeDtypeStruct((batch * seq, H_A * HEAD_DIM), jnp.bfloat16),
        grid=(batch, nb),
        in_specs=[
            pl.BlockSpec((H_A * BLOCK, KV_LORA), lambda b, n: (b * nb + n, 0)),
            pl.BlockSpec((BLOCK, IDX_HEADS * IDX_DIM), lambda b, n: (b * nb + n, 0)),
            pl.BlockSpec((BLOCK, LANE), lambda b, n: (b * nb + n, kw_blk)),
            pl.BlockSpec((seq, LANE), lambda b, n: (b, kw_blk)),
            pl.BlockSpec((seq, KV_LORA), lambda b, n: (b, 0)),
            pl.BlockSpec((None, KVT_ROWS, seq), lambda b, n: (b, 0, 0)),
            pl.BlockSpec(gtab.shape, lambda b, n: (0, 0), pipeline_mode=pl.Buffered(1)),
            pl.BlockSpec(wuv2.shape, lambda b, n: (0, 0, 0)),
        ],
        out_specs=pl.BlockSpec((BLOCK, H_A * HEAD_DIM), lambda b, n: (b * nb + n, 0)),
        scratch_shapes=[
            pltpu.VMEM((seq, LANE), jnp.bfloat16),
            pltpu.VMEM((seq, LANE), jnp.bfloat16),
            pltpu.VMEM((seq, BLOCK), jnp.int32),
            pltpu.VMEM((seq, H_A // DSA_GROUPS * BLOCK), jnp.float32),
            pltpu.VMEM((KVT_ROWS, H_A // DSA_GROUPS * BLOCK), jnp.float32),
        ],
        compiler_params=_cp(("parallel", "arbitrary")),
        name="dsa_attn",
    )(q, qi, proj, proj, kv, kvt, gtab, wuv2)


def _swa_kernel(q_ref, kc_ref, kp_ref, vc_ref, vp_ref, gq_ref, gk_ref, bias_ref, sink_ref, o_ref):
    n = pl.program_id(1)
    g = H_B // KV_B
    scale = HEAD_DIM ** -0.5
    kn = _pair_norm(jnp.concatenate([kp_ref[...], kc_ref[...]], axis=0), gk_ref[...])
    k_nat = kn.astype(jnp.bfloat16)
    k_swap = pltpu.roll(kn, HEAD_DIM, axis=1).astype(jnp.bfloat16)
    v = jnp.concatenate([vp_ref[...], vc_ref[...]], axis=0).astype(jnp.bfloat16)
    col = lax.broadcasted_iota(jnp.int32, (BLOCK, 2 * BLOCK), 1)
    first = jnp.where((n == 0) & (col < BLOCK), NEG, 0.0)
    lo = lax.broadcasted_iota(jnp.int32, (BLOCK, LANE), 1) < HEAD_DIM
    for p in range(H_B // 2):
        hk = (2 * p) // g
        qn = _pair_norm(q_ref[:, p * LANE:(p + 1) * LANE], gq_ref[...]) * scale
        halves = []
        for half in range(2):
            head = 2 * p + half
            qh = (jnp.where(lo, qn, 0.0) if half == 0 else jnp.where(lo, 0.0, qn)).astype(jnp.bfloat16)
            kh = k_nat if half == hk else k_swap
            s = lax.dot_general(qh, kh, _NT, preferred_element_type=jnp.float32) + bias_ref[head] + first
            sink = sink_ref[head]
            mx = jnp.maximum(jnp.max(s, axis=-1, keepdims=True), sink)
            e = jnp.exp(s - mx)
            den = jnp.sum(e, axis=-1, keepdims=True) + jnp.exp(sink - mx)
            o = jnp.dot((e / den).astype(jnp.bfloat16), v, preferred_element_type=jnp.float32)
            halves.append(o if half == hk else pltpu.roll(o, HEAD_DIM, axis=1))
        o_ref[:, p * LANE:(p + 1) * LANE] = jnp.where(lo, halves[0], halves[1]).astype(o_ref.dtype)


def swa_attn(proj, gq, gk, bias_tab, sinks, *, batch, seq):
    nb = seq // BLOCK
    k_blk = 1920 // LANE
    v_blk = 2048 // LANE
    prev = lambda b, n: b * nb + jnp.maximum(n - 1, 0)
    return pl.pallas_call(
        _swa_kernel,
        out_shape=jax.ShapeDtypeStruct((batch * seq, H_B * HEAD_DIM), jnp.bfloat16),
        grid=(batch, nb),
        in_specs=[
            pl.BlockSpec((BLOCK, H_B * HEAD_DIM), lambda b, n: (b * nb + n, 0)),
            pl.BlockSpec((BLOCK, LANE), lambda b, n: (b * nb + n, k_blk)),
            pl.BlockSpec((BLOCK, LANE), lambda b, n: (prev(b, n), k_blk)),
            pl.BlockSpec((BLOCK, LANE), lambda b, n: (b * nb + n, v_blk)),
            pl.BlockSpec((BLOCK, LANE), lambda b, n: (prev(b, n), v_blk)),
            pl.BlockSpec((1, LANE), lambda b, n: (0, 0)),
            pl.BlockSpec((1, LANE), lambda b, n: (0, 0)),
            pl.BlockSpec(bias_tab.shape, lambda b, n: (0, 0, 0)),
            pl.BlockSpec(memory_space=pltpu.SMEM),
        ],
        out_specs=pl.BlockSpec((BLOCK, H_B * HEAD_DIM), lambda b, n: (b * nb + n, 0)),
        compiler_params=_cp(("parallel", "parallel")),
        name="swa_attn",
    )(proj, proj, proj, proj, proj, jnp.tile(gq, 2).reshape(1, LANE), jnp.tile(gk, 2).reshape(1, LANE),
      bias_tab, sinks)


def _top16(v):
    return pltpu.bitcast(pltpu.bitcast(v, jnp.int32) & jnp.int32(-65536), jnp.float32)


def _split3(c):
    hi = _top16(c)
    r = c - hi
    mid = _top16(r)
    return hi, mid, r - mid


def _head_inv_rms(x_ref, e_ref):
    ss = jnp.zeros((x_ref.shape[0], LANE), jnp.float32)
    for p in range(x_ref.shape[1] // LANE):
        x = x_ref[:, p * LANE:(p + 1) * LANE]
        x2 = x * x
        hi = _top16(x2)
        e_p = e_ref[p * LANE:(p + 1) * LANE, :]
        ss = ss + jnp.dot(hi.astype(jnp.bfloat16), e_p, preferred_element_type=jnp.float32)
        ss = ss + jnp.dot((x2 - hi).astype(jnp.bfloat16), e_p, preferred_element_type=jnp.float32)
    return jnp.concatenate([t.astype(jnp.bfloat16) for t in _split3(lax.rsqrt(ss / HEAD_DIM + EPS))], axis=1)


def _fox_prep_kernel(q_ref, k_ref, v_ref, f_ref, gq_ref, gk_ref, fb_ref, e_ref, et_ref, qo_ref, ko_ref, vt_ref,
                     carry_ref):
    tm = q_ref.shape[0]

    @pl.when(pl.program_id(1) == 0)
    def _():
        carry_ref[...] = jnp.zeros_like(carry_ref)

    z = f_ref[...] + fb_ref[...]
    logf = jnp.minimum(z, 0.0) - jnp.log1p(jnp.exp(-jnp.abs(z)))
    r = lax.broadcasted_iota(jnp.int32, (tm, tm), 0)
    c = lax.broadcasted_iota(jnp.int32, (tm, tm), 1)
    tri = jnp.where(c <= r, 1.0, 0.0)
    cum = jnp.dot(tri, logf, preferred_element_type=jnp.float32, precision=lax.Precision.HIGHEST) + carry_ref[...]
    carry_ref[...] = cum[tm - 1:tm, :]

    inv_q = _head_inv_rms(q_ref, e_ref)
    inv_k = _head_inv_rms(k_ref, e_ref)

    lane = lax.broadcasted_iota(jnp.int32, (tm, LANE), 1)
    masks = []
    for half in range(2):
        al = lane - (HEAD_DIM if half == 0 else 0)
        first, second = (al >= 0) & (al < 3), (al >= 3) & (al < 6)
        masks.append(((al == 0) | (al == 3), (al == 1) | (al == 4), first, second,
                      jnp.where(second, 1.0, 0.0), jnp.where(first, 1.0, 0.0),
                      (lane < HEAD_DIM) if half == 0 else (lane >= HEAD_DIM)))
    for p in range(H_C // 2):
        sl = slice(p * LANE, (p + 1) * LANE)
        qn = q_ref[:, sl] * jnp.dot(inv_q, et_ref[:, sl], preferred_element_type=jnp.float32) * gq_ref[:, sl]
        kn = k_ref[:, sl] * jnp.dot(inv_k, et_ref[:, sl], preferred_element_type=jnp.float32) * gk_ref[:, sl]
        for half in range(2):
            h = 2 * p + half
            is_hi, is_mid, first, second, ones_q, ones_k, own = masks[half]
            hi, mid, lo = _split3(jnp.broadcast_to(cum[:, h:h + 1], (tm, LANE)))
            part = jnp.where(is_hi, hi, jnp.where(is_mid, mid, lo))
            qo_ref[:, h * LANE:(h + 1) * LANE] = jnp.where(
                own, qn, jnp.where(first, part, ones_q)).astype(qo_ref.dtype)
            ko_ref[:, h * LANE:(h + 1) * LANE] = jnp.where(
                own, kn, jnp.where(second, -part, ones_k)).astype(ko_ref.dtype)
    ones = jnp.ones((V_ROWS - HEAD_DIM, tm), vt_ref.dtype)
    for p in range(H_C // 2):
        vt = v_ref[:, p * LANE:(p + 1) * LANE].T.astype(vt_ref.dtype)
        for half in range(2):
            h = 2 * p + half
            vt_ref[h * V_ROWS:h * V_ROWS + HEAD_DIM, :] = vt[half * HEAD_DIM:(half + 1) * HEAD_DIM, :]
            vt_ref[h * V_ROWS + HEAD_DIM:(h + 1) * V_ROWS, :] = ones


def fox_prep(proj, gq, gk, fbias, *, batch, seq, tm):
    m = proj.shape[0]
    hd = H_C * HEAD_DIM
    nt = seq // tm
    f_blk = 3 * hd // LANE
    fb = jnp.zeros((1, LANE), jnp.float32).at[0, :H_C].set(fbias)
    aug = jax.ShapeDtypeStruct((m, H_C * LANE), jnp.bfloat16)
    row = lambda b, i: b * nt + i
    e_np = (np.arange(hd)[:, None] // HEAD_DIM == np.arange(LANE)[None, :]).astype(np.float32)
    e = jnp.asarray(e_np, jnp.bfloat16)
    et = jnp.asarray(np.tile(e_np.T, (3, 1)), jnp.bfloat16)
    return pl.pallas_call(
        _fox_prep_kernel,
        out_shape=(aug, aug, jax.ShapeDtypeStruct((batch, H_C * V_ROWS, seq), jnp.bfloat16)),
        grid=(batch, nt),
        in_specs=[
            pl.BlockSpec((tm, hd), lambda b, i: (row(b, i), 0)),
            pl.BlockSpec((tm, hd), lambda b, i: (row(b, i), 1)),
            pl.BlockSpec((tm, hd), lambda b, i: (row(b, i), 2)),
            pl.BlockSpec((tm, LANE), lambda b, i: (row(b, i), f_blk)),
            pl.BlockSpec((1, hd), lambda b, i: (0, 0)),
            pl.BlockSpec((1, hd), lambda b, i: (0, 0)),
            pl.BlockSpec((1, LANE), lambda b, i: (0, 0)),
            pl.BlockSpec((hd, LANE), lambda b, i: (0, 0)),
            pl.BlockSpec((3 * LANE, hd), lambda b, i: (0, 0)),
        ],
        out_specs=(
            pl.BlockSpec((tm, H_C * LANE), lambda b, i: (row(b, i), 0)),
            pl.BlockSpec((tm, H_C * LANE), lambda b, i: (row(b, i), 0)),
            pl.BlockSpec((None, H_C * V_ROWS, tm), lambda b, i: (b, 0, i)),
        ),
        scratch_shapes=[pltpu.VMEM((1, LANE), jnp.float32)],
        compiler_params=_cp(("parallel", "arbitrary")),
        name="fox_prep",
    )(proj, proj, proj, proj, jnp.tile(gq * HEAD_DIM ** -0.5, H_C).reshape(1, hd), jnp.tile(gk, H_C).reshape(1, hd),
      fb, e, et)


def _fox_attn_kernel(tq, q_ref, k_ref, vt_ref, o_ref, s_ref, p_ref):
    seq = q_ref.shape[0]
    nq = seq // tq
    diag_ok = (lax.broadcasted_iota(jnp.int32, (tq, tq), 0) <= lax.broadcasted_iota(jnp.int32, (tq, tq), 1))

    def logits(i):
        nk = (i + 1) * tq
        for hh in range(2):
            hl = slice(hh * LANE, (hh + 1) * LANE)
            s = lax.dot_general(k_ref[0:nk, hl], q_ref[i * tq:nk, hl], _NT,
                                preferred_element_type=jnp.float32)
            if i > 0:
                s_ref[i % 2, hh, 0:i * tq, :] = s[:i * tq]
            s_ref[i % 2, hh, i * tq:nk, :] = jnp.where(diag_ok, s[i * tq:], NEG)

    def probs(i):
        nk = (i + 1) * tq
        for hh in range(2):
            s = s_ref[i % 2, hh, 0:nk, :]
            p_ref[i % 2, hh, 0:nk, :] = jnp.exp(s - jnp.max(s, axis=0, keepdims=True)).astype(p_ref.dtype)

    def values(i):
        nk = (i + 1) * tq
        outs = []
        for hh in range(2):
            acc = jnp.dot(vt_ref[hh * V_ROWS:(hh + 1) * V_ROWS, 0:nk], p_ref[i % 2, hh, 0:nk, :],
                          preferred_element_type=jnp.float32)
            outs.append(acc[:HEAD_DIM] / acc[HEAD_DIM:HEAD_DIM + 1])
        o_ref[i * tq:nk, :] = jnp.concatenate(outs, axis=0).T.astype(o_ref.dtype)

    logits(0)
    for i in range(nq):
        if i > 0:
            values(i - 1)
        if i + 1 < nq:
            logits(i + 1)
        probs(i)
    values(nq - 1)


def fox_attn(qa, ka, vt, *, batch, seq, tq):
    pairs = H_C // 2
    return pl.pallas_call(
        functools.partial(_fox_attn_kernel, tq),
        out_shape=jax.ShapeDtypeStruct((batch * seq, H_C * HEAD_DIM), jnp.bfloat16),
        grid=(batch, pairs),
        in_specs=[
            pl.BlockSpec((seq, 2 * LANE), lambda b, c: (b, c)),
            pl.BlockSpec((seq, 2 * LANE), lambda b, c: (b, c)),
            pl.BlockSpec((None, 2 * V_ROWS, seq), lambda b, c: (b, c, 0)),
        ],
        out_specs=pl.BlockSpec((seq, 2 * HEAD_DIM), lambda b, c: (b, c)),
        scratch_shapes=[pltpu.VMEM((2, 2, seq, tq), jnp.float32), pltpu.VMEM((2, 2, seq, tq), jnp.bfloat16)],
        compiler_params=_cp(("parallel", "parallel")),
        name="fox_attn",
    )(qa, ka, vt)


def _rel_bucket(dist):
    n = jnp.maximum(dist, 0)
    exact = N_BUCKETS // 2
    nf = jnp.maximum(n, 1).astype(jnp.float32)
    large = exact + (jnp.log(nf / exact) / math.log(MAX_DISTANCE / exact) * (N_BUCKETS - exact)).astype(jnp.int32)
    large = jnp.minimum(large, N_BUCKETS - 1)
    return jnp.where(n < exact, n, large)


def _toeplitz(v, rows, cols, off):
    h, dmax = v.shape
    lt = cols + rows - 1
    idx = off + rows - 1 - np.arange(lt)
    e = jnp.where(((idx >= 0) & (idx < dmax))[None], v[:, np.clip(idx, 0, dmax - 1)], 0.0)
    f = jnp.pad(e, ((0, 0), (0, 1)))
    a = jnp.tile(f, (1, rows))[:, :rows * lt].reshape(h, rows, lt)
    return a[:, :, rows - 1:rows - 1 + cols]


def kernel(x, p, attn_norm, ffn_norm, ple_norm, rel_bias, w_in_even, a_cq_norm, a_ckv_norm, a_w_uq, a_q_norm,
           a_w_qidx, a_w_uv, b_q_norm, b_k_norm, b_sinks, w_out_even, w_in_odd, c_forget_bias, c_q_norm,
           c_k_norm, w_out_odd, w_up, ffn_conv, w_down, w_ple_gate, w_ple_proj):
    batch, seq, d = x.shape
    depth = p.shape[0]
    m = batch * seq
    bf = jnp.bfloat16
    d_ff = w_down.shape[1]
    f_pad = -(-d_ff // 512) * 512
    hd_c = H_C * HEAD_DIM

    bias_by_dist = rel_bias[_rel_bucket(jnp.arange(seq))].T
    g_off = seq - BLOCK
    g_cols = g_off + KEY_TILE
    gtab = jnp.transpose(_toeplitz(bias_by_dist[:H_A], BLOCK, g_cols, g_off), (2, 0, 1)).reshape(g_cols, H_A * BLOCK)
    dist_b = np.arange(BLOCK)[:, None] + BLOCK - np.arange(2 * BLOCK)[None, :]
    band = (dist_b >= 0) & (dist_b < WINDOW)
    btab = jnp.where(band[None], _toeplitz(bias_by_dist[H_A:, :WINDOW], BLOCK, 2 * BLOCK, BLOCK), NEG)

    o1 = Q_LORA
    o2 = o1 + KV_LORA
    o3 = o2 + IDX_DIM
    o4 = o3 + IDX_HEADS
    o5 = o4 + H_B * HEAD_DIM
    o6 = o5 + KV_B * HEAD_DIM

    w_in_e = cast_cols(w_in_even, [(o4, o5 - o4, 0), (0, o1, 1024), (o1, o2 - o1, 1536), (o2, o4 - o2, 1792),
                                   (o5, o6 - o5, 1920), (o6, w_in_even.shape[2] - o6, 2048)], 2304, tr=512)
    w_in_o = cast_cols(w_in_odd, [(0, w_in_odd.shape[2], 0)], 3 * hd_c + LANE, tr=256)
    w_uq = cast_cols(a_w_uq, [(0, a_w_uq.shape[2], 0)], a_w_uq.shape[2], tr=256)
    w_qi = cast_cols(a_w_qidx, [(0, a_w_qidx.shape[2], 0)], a_w_qidx.shape[2], tr=512)
    w_out_e = cast_cols(w_out_even, [(0, d, 0)], d, tr=512)
    w_out_o = cast_cols(w_out_odd, [(0, d, 0)], d, tr=512)
    wgu = cast_cols(w_up, [(0, d_ff, 0), (d_ff, d_ff, f_pad)], 2 * f_pad, tr=128)
    cgu = jnp.pad(ffn_conv.reshape(-1, CONV_WIDTH, 2, d_ff),
                  ((0, 0), (0, 0), (0, 0), (0, f_pad - d_ff))).reshape(-1, CONV_WIDTH, 2 * f_pad)
    wd = cast_rows(w_down, f_pad, tr=LANE)
    w_pg = cast_cols(w_ple_gate, [(0, d, 0)], d, tr=512)
    w_pp = cast_cols(w_ple_proj, [(0, d, 0)], d, tr=w_ple_proj.shape[1])
    p = p.reshape(depth, m, -1)
    wuv = jnp.swapaxes(a_w_uv, 2, 3).reshape(-1, H_A // 2, 2, HEAD_DIM, KV_LORA)
    zero = jnp.zeros_like(wuv[:, :, 0])
    wuv2 = jnp.concatenate([jnp.concatenate([wuv[:, :, 0], zero], axis=3),
                            jnp.concatenate([zero, wuv[:, :, 1]], axis=3)], axis=2).astype(bf)

    x = x.reshape(m, d)
    for i in range(depth):
        if i % 2 == 0:
            e = i // 2
            proj = norm_matmul(x, attn_norm[i], w_in_e, e, tm=1024, tn=w_in_e.shape[2] // 3)
            q, qi, kv, kvt = dsa_prep(proj, a_cq_norm[e], a_ckv_norm[e], a_q_norm[e], w_uq, w_qi, e, seq=seq, tm=256)
            y_a = dsa_attn(q, qi, proj, kv, kvt, gtab, wuv2[e], batch=batch, seq=seq)
            y_b = swa_attn(proj, b_q_norm[e], b_k_norm[e], btab, b_sinks[e], batch=batch, seq=seq)
            x = matmul_res([y_a, y_b], w_out_e, e, x, tm=1024, tn=512)
        else:
            o = i // 2
            proj = norm_matmul(x, attn_norm[i], w_in_o, o, tm=1024, tn=w_in_o.shape[2] // 7)
            qa, ka, vt = fox_prep(proj, c_q_norm[o], c_k_norm[o], c_forget_bias[o], batch=batch, seq=seq, tm=256)
            y = fox_attn(qa, ka, vt, batch=batch, seq=seq, tq=256)
            x = matmul_res([y], w_out_o, o, x, tm=1024, tn=512)
        x = conv_ffn(x, ffn_norm[i], wgu, cgu, wd, i, seq=seq, tm=1024, tf=512)
        x = ple(x, ple_norm[i], p, w_pg, w_pp, i, tm=1024, tn=512)
    return x.reshape(batch, seq, d)
```

```python
import functools
import math

import jax
import jax.numpy as jnp
import numpy as np
from jax import lax
from jax.experimental import pallas as pl
from jax.experimental.pallas import tpu as pltpu

HEAD_DIM = 64
BLOCK = 128
EPS = 1e-6
H_A = 16
Q_LORA = 512
KV_LORA = 256
IDX_HEADS = 16
IDX_DIM = 64
TOPK_MAX = 256
TOPK_DIV = 4
H_B = 16
KV_B = 2
WINDOW = 128
H_C = 32
N_BUCKETS = 32
MAX_DISTANCE = 1024
CONV_WIDTH = 3

LANE = 128
NEG = -1e30
KEY_TILE = 512
HALO = 16
DSA_GROUPS = 2
KVT_ROWS = KV_LORA + 16
V_ROWS = HEAD_DIM + 16
VMEM_LIMIT = 56 * 1024 * 1024
INT_MIN = -2 ** 31

_NT = (((1,), (1,)), ((), ()))


def _cp(sem, vmem=VMEM_LIMIT):
    return pltpu.CompilerParams(dimension_semantics=sem, vmem_limit_bytes=vmem)


def _rms(x, g):
    return x * lax.rsqrt(jnp.mean(x * x, axis=-1, keepdims=True) + EPS) * g


def _pair_norm(x, g2):
    lo = lax.broadcasted_iota(jnp.int32, x.shape, 1) < HEAD_DIM
    x2 = x * x
    s_lo = jnp.sum(jnp.where(lo, x2, 0.0), axis=-1, keepdims=True)
    s_hi = jnp.sum(jnp.where(lo, 0.0, x2), axis=-1, keepdims=True)
    inv = jnp.where(lo, lax.rsqrt(s_lo / HEAD_DIM + EPS), lax.rsqrt(s_hi / HEAD_DIM + EPS))
    return x * inv * g2


def _cast_cols_kernel(segments, x_ref, o_ref):
    o_ref[...] = jnp.zeros_like(o_ref)
    for src, length, dst in segments:
        o_ref[:, dst:dst + length] = x_ref[:, src:src + length].astype(o_ref.dtype)


def cast_cols(w, segments, out_c, *, tr):
    l, r, c = w.shape
    return pl.pallas_call(
        functools.partial(_cast_cols_kernel, tuple(segments)),
        out_shape=jax.ShapeDtypeStruct((l, r, out_c), jnp.bfloat16),
        grid=(l, r // tr),
        in_specs=[pl.BlockSpec((None, tr, c), lambda a, b: (a, b, 0))],
        out_specs=pl.BlockSpec((None, tr, out_c), lambda a, b: (a, b, 0)),
        compiler_params=_cp(("parallel", "parallel")),
        name="cast_cols",
    )(w)


def _cast_rows_kernel(n_src, x_ref, o_ref):
    r = pl.program_id(1)

    @pl.when(r < n_src)
    def _():
        o_ref[...] = x_ref[...].astype(o_ref.dtype)

    @pl.when(r >= n_src)
    def _():
        o_ref[...] = jnp.zeros_like(o_ref)


def cast_rows(w, out_r, *, tr):
    l, r, c = w.shape
    n_src = r // tr
    return pl.pallas_call(
        functools.partial(_cast_rows_kernel, n_src),
        out_shape=jax.ShapeDtypeStruct((l, out_r, c), jnp.bfloat16),
        grid=(l, out_r // tr),
        in_specs=[pl.BlockSpec((None, tr, c), lambda a, b: (a, jnp.minimum(b, n_src - 1), 0))],
        out_specs=pl.BlockSpec((None, tr, c), lambda a, b: (a, b, 0)),
        compiler_params=_cp(("parallel", "parallel")),
        name="cast_rows",
    )(w)


def _norm_matmul_kernel(x_ref, g_ref, w_ref, o_ref, h_ref):
    @pl.when(pl.program_id(1) == 0)
    def _():
        h_ref[...] = _rms(x_ref[...], g_ref[...]).astype(h_ref.dtype)

    o_ref[...] = jnp.dot(h_ref[...], w_ref[...], preferred_element_type=jnp.float32)


def norm_matmul(x, g, w, layer, *, tm, tn):
    m, d = x.shape
    n = w.shape[2]
    return pl.pallas_call(
        _norm_matmul_kernel,
        out_shape=jax.ShapeDtypeStruct((m, n), jnp.float32),
        grid=(m // tm, n // tn),
        in_specs=[
            pl.BlockSpec((tm, d), lambda i, j: (i, 0)),
            pl.BlockSpec((1, d), lambda i, j: (0, 0)),
            pl.BlockSpec((None, d, tn), lambda i, j: (layer, 0, j)),
        ],
        out_specs=pl.BlockSpec((tm, tn), lambda i, j: (i, j)),
        scratch_shapes=[pltpu.VMEM((tm, d), jnp.bfloat16)],
        compiler_params=_cp(("parallel", "arbitrary")),
        name="norm_matmul",
    )(x, g.reshape(1, d), w)


def _matmul_res_kernel(n_pairs, *refs):
    x_ref = refs[2 * n_pairs]
    o_ref = refs[2 * n_pairs + 1]
    acc = x_ref[...]
    for p in range(n_pairs):
        acc = acc + jnp.dot(refs[2 * p][...], refs[2 * p + 1][...], preferred_element_type=jnp.float32)
    o_ref[...] = acc


def matmul_res(acts, w, layer, x, *, tm, tn):
    m, n = x.shape
    in_specs, args = [], []
    for r, a in enumerate(acts):
        k = a.shape[1]
        in_specs += [pl.BlockSpec((tm, k), lambda i, j: (i, 0)),
                     pl.BlockSpec((None, k, tn), lambda i, j, r=r: (layer, r, j))]
        args += [a, w]
    in_specs.append(pl.BlockSpec((tm, tn), lambda i, j: (i, j)))
    return pl.pallas_call(
        functools.partial(_matmul_res_kernel, len(acts)),
        out_shape=jax.ShapeDtypeStruct((m, n), jnp.float32),
        grid=(m // tm, n // tn),
        in_specs=in_specs,
        out_specs=pl.BlockSpec((tm, tn), lambda i, j: (i, j)),
        compiler_params=_cp(("parallel", "parallel")),
        name="matmul_res",
    )(*args, x)


def _ffn_kernel(tiles_per_seq, x_ref, xh_ref, g_ref, wg_ref, wu_ref, cg_ref, cu_ref, wd_ref, o_ref, h_ref):
    i = pl.program_id(0)
    f = pl.program_id(1)

    @pl.when(f == 0)
    def _():
        seq_start = (i % tiles_per_seq) == 0
        hh = _rms(xh_ref[...], g_ref[...])
        h_ref[0:HALO, :] = jnp.where(seq_start, 0.0, hh).astype(h_ref.dtype)
        h_ref[HALO:, :] = _rms(x_ref[...], g_ref[...]).astype(h_ref.dtype)
        o_ref[...] = x_ref[...]

    h = h_ref[...]

    def conv(z, c_ref):
        z1 = pltpu.roll(z, 1, axis=0)
        z2 = pltpu.roll(z, 2, axis=0)
        y = c_ref[2:3, :] * z + c_ref[1:2, :] * z1 + c_ref[0:1, :] * z2
        return y[HALO:, :]

    gate = conv(jnp.dot(h, wg_ref[...], preferred_element_type=jnp.float32), cg_ref)
    up = conv(jnp.dot(h, wu_ref[...], preferred_element_type=jnp.float32), cu_ref)
    a = (gate * jax.nn.sigmoid(gate) * up).astype(jnp.bfloat16)
    o_ref[...] += jnp.dot(a, wd_ref[...], preferred_element_type=jnp.float32)


def conv_ffn(x, g, wgu, cgu, wd, layer, *, seq, tm, tf):
    m, d = x.shape
    hb = tm // HALO
    nf = wd.shape[1] // tf
    return pl.pallas_call(
        functools.partial(_ffn_kernel, seq // tm),
        out_shape=jax.ShapeDtypeStruct((m, d), jnp.float32),
        grid=(m // tm, nf),
        in_specs=[
            pl.BlockSpec((tm, d), lambda i, f: (i, 0), pipeline_mode=pl.Buffered(1)),
            pl.BlockSpec((HALO, d), lambda i, f: (jnp.maximum(i * hb - 1, 0), 0)),
            pl.BlockSpec((1, d), lambda i, f: (0, 0)),
            pl.BlockSpec((None, d, tf), lambda i, f: (layer, 0, f)),
            pl.BlockSpec((None, d, tf), lambda i, f: (layer, 0, nf + f)),
            pl.BlockSpec((None, CONV_WIDTH, tf), lambda i, f: (layer, 0, f)),
            pl.BlockSpec((None, CONV_WIDTH, tf), lambda i, f: (layer, 0, nf + f)),
            pl.BlockSpec((None, tf, d), lambda i, f: (layer, f, 0)),
        ],
        out_specs=pl.BlockSpec((tm, d), lambda i, f: (i, 0)),
        scratch_shapes=[pltpu.VMEM((tm + HALO, d), jnp.bfloat16)],
        compiler_params=_cp(("parallel", "arbitrary")),
        name="conv_ffn",
    )(x, x, g.reshape(1, d), wgu, wgu, cgu, cgu, wd)


def _ple_kernel(x_ref, g_ref, p_ref, wg_ref, wp_ref, o_ref, h_ref):
    j = pl.program_id(1)
    tn = o_ref.shape[1]

    @pl.when(j == 0)
    def _():
        h_ref[...] = _rms(x_ref[...], g_ref[...]).astype(h_ref.dtype)

    gate = jax.nn.sigmoid(jnp.dot(h_ref[...], wg_ref[...], preferred_element_type=jnp.float32))
    pp = jnp.dot(p_ref[...].astype(jnp.bfloat16), wp_ref[...], preferred_element_type=jnp.float32)
    o_ref[...] = x_ref[:, pl.ds(pl.multiple_of(j * tn, LANE), tn)] + gate * pp


def ple(x, g, p, wg, wp, layer, *, tm, tn):
    m, d = x.shape
    pd = p.shape[2]
    return pl.pallas_call(
        _ple_kernel,
        out_shape=jax.ShapeDtypeStruct((m, d), jnp.float32),
        grid=(m // tm, d // tn),
        in_specs=[
            pl.BlockSpec((tm, d), lambda i, j: (i, 0)),
            pl.BlockSpec((1, d), lambda i, j: (0, 0)),
            pl.BlockSpec((None, tm, pd), lambda i, j: (layer, i, 0)),
            pl.BlockSpec((None, d, tn), lambda i, j: (layer, 0, j)),
            pl.BlockSpec((None, pd, tn), lambda i, j: (layer, 0, j)),
        ],
        out_specs=pl.BlockSpec((tm, tn), lambda i, j: (i, j)),
        scratch_shapes=[pltpu.VMEM((tm, d), jnp.bfloat16)],
        compiler_params=_cp(("parallel", "arbitrary")),
        name="ple",
    )(x, g.reshape(1, d), p, wg, wp)


def _dsa_prep_kernel(cq_ref, ckv_ref, gcq_ref, gckv_ref, gq_ref, wuq_ref, wqi_ref, q_ref, qi_ref, kv_ref, kvt_ref):
    tm = cq_ref.shape[0]
    cq = _rms(cq_ref[...], gcq_ref[...]).astype(jnp.bfloat16)
    kv = _rms(ckv_ref[...], gckv_ref[...])
    kv_ref[...] = kv.astype(kv_ref.dtype)
    kvt_ref[0:KV_LORA, :] = kv.T.astype(kvt_ref.dtype)
    kvt_ref[KV_LORA:, :] = jnp.ones((KVT_ROWS - KV_LORA, tm), kvt_ref.dtype)
    qi_ref[...] = jnp.dot(cq, wqi_ref[...], preferred_element_type=jnp.float32).astype(qi_ref.dtype)
    att_scale = KV_LORA ** -0.5
    for h in range(H_A):
        ql = jnp.dot(cq, wuq_ref[:, h * KV_LORA:(h + 1) * KV_LORA], preferred_element_type=jnp.float32)
        qn = (_rms(ql, gq_ref[...]) * att_scale).astype(q_ref.dtype)
        for r in range(tm // BLOCK):
            q_ref[(r * H_A + h) * BLOCK:(r * H_A + h + 1) * BLOCK, :] = qn[r * BLOCK:(r + 1) * BLOCK, :]


def dsa_prep(proj, gcq, gckv, gq, wuq, wqi, layer, *, seq, tm):
    m = proj.shape[0]
    cq_blk = 1024 // Q_LORA
    ckv_blk = 1536 // KV_LORA
    nt = seq // tm
    return pl.pallas_call(
        _dsa_prep_kernel,
        out_shape=(
            jax.ShapeDtypeStruct((m * H_A, KV_LORA), jnp.bfloat16),
            jax.ShapeDtypeStruct((m, IDX_HEADS * IDX_DIM), jnp.bfloat16),
            jax.ShapeDtypeStruct((m, KV_LORA), jnp.bfloat16),
            jax.ShapeDtypeStruct((m // seq, KVT_ROWS, seq), jnp.bfloat16),
        ),
        grid=(m // tm,),
        in_specs=[
            pl.BlockSpec((tm, Q_LORA), lambda i: (i, cq_blk)),
            pl.BlockSpec((tm, KV_LORA), lambda i: (i, ckv_blk)),
            pl.BlockSpec((1, Q_LORA), lambda i: (0, 0)),
            pl.BlockSpec((1, KV_LORA), lambda i: (0, 0)),
            pl.BlockSpec((1, KV_LORA), lambda i: (0, 0)),
            pl.BlockSpec((None, Q_LORA, H_A * KV_LORA), lambda i: (layer, 0, 0)),
            pl.BlockSpec((None, Q_LORA, IDX_HEADS * IDX_DIM), lambda i: (layer, 0, 0)),
        ],
        out_specs=(
            pl.BlockSpec((tm * H_A, KV_LORA), lambda i: (i, 0)),
            pl.BlockSpec((tm, IDX_HEADS * IDX_DIM), lambda i: (i, 0)),
            pl.BlockSpec((tm, KV_LORA), lambda i: (i, 0)),
            pl.BlockSpec((None, KVT_ROWS, tm), lambda i: (i // nt, 0, i % nt)),
        ),
        compiler_params=_cp(("parallel",)),
        name="dsa_prep",
    )(proj, proj, gcq.reshape(1, -1), gckv.reshape(1, -1), gq.reshape(1, -1), wuq, wqi)


def _dsa_attn_kernel(topk, g_off, q_ref, qi_ref, kwq_ref, kwall_ref, kv_ref, kvt_ref, g_ref, wuv_ref, o_ref,
                     ka_ref, kb_ref, key_ref, s_ref, acc_ref, cut_ref):
    n = pl.program_id(1)
    seq_len = g_off + BLOCK
    per_tile = KEY_TILE // BLOCK
    n_tiles = (n + per_tile) // per_tile
    q_pos = n * BLOCK + lax.broadcasted_iota(jnp.int32, (KEY_TILE, BLOCK), 1)
    k_row = lax.broadcasted_iota(jnp.int32, (KEY_TILE, BLOCK), 0)

    @pl.when(n == 0)
    def _():
        kw = kwall_ref[...]
        lo = lax.broadcasted_iota(jnp.int32, kw.shape, 1) < IDX_DIM
        ka_ref[...] = jnp.where(lo, kw, 0.0).astype(ka_ref.dtype)
        kb_ref[...] = jnp.where(lo, 0.0, pltpu.roll(kw, IDX_DIM, axis=1)).astype(kb_ref.dtype)

    w_t = kwq_ref[...].T

    def score_tile(j, carry):
        col = pl.multiple_of(j * KEY_TILE, KEY_TILE)
        ka = ka_ref[pl.ds(col, KEY_TILE), :]
        kb = kb_ref[pl.ds(col, KEY_TILE), :]
        s = jnp.zeros((KEY_TILE, BLOCK), jnp.float32)
        for p in range(IDX_HEADS // 2):
            qp = qi_ref[:, p * LANE:(p + 1) * LANE]
            da = lax.dot_general(ka, qp, _NT, preferred_element_type=jnp.float32)
            db = lax.dot_general(kb, qp, _NT, preferred_element_type=jnp.float32)
            s = s + w_t[IDX_DIM + 2 * p:IDX_DIM + 2 * p + 1, :] * jnp.maximum(da, 0.0)
            s = s + w_t[IDX_DIM + 2 * p + 1:IDX_DIM + 2 * p + 2, :] * jnp.maximum(db, 0.0)
        s = jnp.where(col + k_row <= q_pos, s, -jnp.inf)
        bits = pltpu.bitcast(s, jnp.int32)
        key_ref[pl.ds(col, KEY_TILE), :] = jnp.where(bits < 0, bits ^ jnp.int32(0x7FFFFFFF), bits)
        return carry

    lax.fori_loop(0, n_tiles, score_tile, 0)

    def bit_step(b, tu):
        cand_u = tu | lax.shift_left(jnp.int32(1), jnp.int32(31) - b)
        cand_s = cand_u ^ jnp.int32(INT_MIN)

        def count_tile(j, cnt):
            col = pl.multiple_of(j * KEY_TILE, KEY_TILE)
            hit = jnp.where(key_ref[pl.ds(col, KEY_TILE), :] >= cand_s, 1, 0)
            return cnt + jnp.sum(hit.reshape(KEY_TILE // 8, 8, BLOCK), axis=0)

        cnt = lax.fori_loop(0, n_tiles, count_tile, jnp.zeros((8, BLOCK), jnp.int32))
        total = jnp.sum(cnt, axis=0, keepdims=True)
        return jnp.where(total >= topk, cand_u, tu)

    thr = lax.fori_loop(0, 32, bit_step, jnp.zeros((1, BLOCK), jnp.int32)) ^ jnp.int32(INT_MIN)

    def count_gt_ge(j, c):
        col = pl.multiple_of(j * KEY_TILE, KEY_TILE)
        k = key_ref[pl.ds(col, KEY_TILE), :]
        gt = jnp.sum(jnp.where(k > thr, 1, 0).reshape(KEY_TILE // 8, 8, BLOCK), axis=0)
        ge = jnp.sum(jnp.where(k >= thr, 1, 0).reshape(KEY_TILE // 8, 8, BLOCK), axis=0)
        return c[0] + gt, c[1] + ge

    zero8 = jnp.zeros((8, BLOCK), jnp.int32)
    c_gt, c_ge = lax.fori_loop(0, n_tiles, count_gt_ge, (zero8, zero8))
    excess = jnp.sum(c_ge, axis=0, keepdims=True) > topk
    n_keep = topk - jnp.sum(c_gt, axis=0, keepdims=True)
    cut_ref[...] = jnp.full_like(cut_ref, seq_len)

    @pl.when(jnp.max(jnp.where(excess, 1, 0)) > 0)
    def _():
        def idx_step(b, v):
            cand = v | lax.shift_left(jnp.int32(1), jnp.int32(seq_len.bit_length() - 1) - b)

            def count_tile(j, cnt):
                col = pl.multiple_of(j * KEY_TILE, KEY_TILE)
                hit = (key_ref[pl.ds(col, KEY_TILE), :] == thr) & (col + k_row < cand)
                return cnt + jnp.sum(jnp.where(hit, 1, 0).reshape(KEY_TILE // 8, 8, BLOCK), axis=0)

            cnt = lax.fori_loop(0, n_tiles, count_tile, zero8)
            return jnp.where(jnp.sum(cnt, axis=0, keepdims=True) < n_keep, cand, v)

        v = lax.fori_loop(0, seq_len.bit_length(), idx_step, jnp.zeros((1, BLOCK), jnp.int32))
        cut_ref[...] = jnp.where(excess, v, seq_len)

    cut = cut_ref[...]

    group_w = (H_A // DSA_GROUPS) * BLOCK
    for grp in range(DSA_GROUPS):
        cs = slice(grp * group_w, (grp + 1) * group_w)
        q_grp = q_ref[cs, :]

        def logits_tile(j, m):
            col = pl.multiple_of(j * KEY_TILE, KEY_TILE)
            k = key_ref[pl.ds(col, KEY_TILE), :]
            k_pos = col + k_row
            sel = ((k > thr) | ((k == thr) & (k_pos <= cut))) & (k_pos <= q_pos)
            am = jnp.where(sel, 0.0, NEG)
            s = lax.dot_general(kv_ref[pl.ds(col, KEY_TILE), :], q_grp, _NT, preferred_element_type=jnp.float32)
            grow = pl.multiple_of(g_off - n * BLOCK + col, LANE)
            s = s + (g_ref[pl.ds(grow, KEY_TILE), cs] + jnp.concatenate([am] * (H_A // DSA_GROUPS), axis=1))
            s_ref[pl.ds(col, KEY_TILE), :] = s
            return jnp.maximum(m, jnp.max(s, axis=0, keepdims=True))

        m = lax.fori_loop(0, n_tiles, logits_tile, jnp.full((1, group_w), NEG, jnp.float32))
        acc_ref[...] = jnp.zeros_like(acc_ref)

        def value_tile(j, carry):
            col = pl.multiple_of(j * KEY_TILE, KEY_TILE)
            p = jnp.exp(s_ref[pl.ds(col, KEY_TILE), :] - m).astype(jnp.bfloat16)
            acc_ref[...] += jnp.dot(kvt_ref[:, pl.ds(col, KEY_TILE)], p, preferred_element_type=jnp.float32)
            return carry

        lax.fori_loop(0, n_tiles, value_tile, 0)

        o_t = (acc_ref[0:KV_LORA, :] / acc_ref[KV_LORA:KV_LORA + 1, :]).astype(jnp.bfloat16)
        for pp in range(H_A // DSA_GROUPS // 2):
            p = grp * (H_A // DSA_GROUPS // 2) + pp
            pair = jnp.concatenate([o_t[:, (2 * pp) * BLOCK:(2 * pp + 1) * BLOCK],
                                    o_t[:, (2 * pp + 1) * BLOCK:(2 * pp + 2) * BLOCK]], axis=0)
            y_t = jnp.dot(wuv_ref[p], pair, preferred_element_type=jnp.float32)
            o_ref[:, p * LANE:(p + 1) * LANE] = y_t.T.astype(o_ref.dtype)


def dsa_attn(q, qi, proj, kv, kvt, gtab, wuv2, *, batch, seq):
    nb = seq // BLOCK
    topk = min(TOPK_MAX, seq // TOPK_DIV)
    kw_blk = 1792 // LANE
    g_off = seq - BLOCK
    return pl.pallas_call(
        functools.partial(_dsa_attn_kernel, topk, g_off),
        out_shape=jax.ShapeDtypeStruct((batch * seq, H_A * HEAD_DIM), jnp.bfloat16),
        grid=(batch, nb),
        in_specs=[
            pl.BlockSpec((H_A * BLOCK, KV_LORA), lambda b, n: (b * nb + n, 0)),
            pl.BlockSpec((BLOCK, IDX_HEADS * IDX_DIM), lambda b, n: (b * nb + n, 0)),
            pl.BlockSpec((BLOCK, LANE), lambda b, n: (b * nb + n, kw_blk)),
            pl.BlockSpec((seq, LANE), lambda b, n: (b, kw_blk)),
            pl.BlockSpec((seq, KV_LORA), lambda b, n: (b, 0)),
            pl.BlockSpec((None, KVT_ROWS, seq), lambda b, n: (b, 0, 0)),
            pl.BlockSpec(gtab.shape, lambda b, n: (0, 0), pipeline_mode=pl.Buffered(1)),
            pl.BlockSpec(wuv2.shape, lambda b, n: (0, 0, 0)),
        ],
        out_specs=pl.BlockSpec((BLOCK, H_A * HEAD_DIM), lambda b, n: (b * nb + n, 0)),
        scratch_shapes=[
            pltpu.VMEM((seq, LANE), jnp.bfloat16),
            pltpu.VMEM((seq, LANE), jnp.bfloat16),
            pltpu.VMEM((seq, BLOCK), jnp.int32),
            pltpu.VMEM((seq, H_A // DSA_GROUPS * BLOCK), jnp.float32),
            pltpu.VMEM((KVT_ROWS, H_A // DSA_GROUPS * BLOCK), jnp.float32),
            pltpu.VMEM((1, BLOCK), jnp.int32),
        ],
        compiler_params=_cp(("parallel", "arbitrary")),
        name="dsa_attn",
    )(q, qi, proj, proj, kv, kvt, gtab, wuv2)


def _swa_kernel(q_ref, kc_ref, kp_ref, vc_ref, vp_ref, gq_ref, gk_ref, bias_ref, sink_ref, o_ref):
    n = pl.program_id(1)
    g = H_B // KV_B
    scale = HEAD_DIM ** -0.5
    kn = _pair_norm(jnp.concatenate([kp_ref[...], kc_ref[...]], axis=0), gk_ref[...])
    k_nat = kn.astype(jnp.bfloat16)
    k_swap = pltpu.roll(kn, HEAD_DIM, axis=1).astype(jnp.bfloat16)
    v = jnp.concatenate([vp_ref[...], vc_ref[...]], axis=0).astype(jnp.bfloat16)
    col = lax.broadcasted_iota(jnp.int32, (BLOCK, 2 * BLOCK), 1)
    first = jnp.where((n == 0) & (col < BLOCK), NEG, 0.0)
    lo = lax.broadcasted_iota(jnp.int32, (BLOCK, LANE), 1) < HEAD_DIM
    for p in range(H_B // 2):
        hk = (2 * p) // g
        qn = _pair_norm(q_ref[:, p * LANE:(p + 1) * LANE], gq_ref[...]) * scale
        halves = []
        for half in range(2):
            head = 2 * p + half
            qh = (jnp.where(lo, qn, 0.0) if half == 0 else jnp.where(lo, 0.0, qn)).astype(jnp.bfloat16)
            kh = k_nat if half == hk else k_swap
            s = lax.dot_general(qh, kh, _NT, preferred_element_type=jnp.float32) + bias_ref[head] + first
            sink = sink_ref[head]
            mx = jnp.maximum(jnp.max(s, axis=-1, keepdims=True), sink)
            e = jnp.exp(s - mx)
            den = jnp.sum(e, axis=-1, keepdims=True) + jnp.exp(sink - mx)
            o = jnp.dot((e / den).astype(jnp.bfloat16), v, preferred_element_type=jnp.float32)
            halves.append(o if half == hk else pltpu.roll(o, HEAD_DIM, axis=1))
        o_ref[:, p * LANE:(p + 1) * LANE] = jnp.where(lo, halves[0], halves[1]).astype(o_ref.dtype)


def swa_attn(proj, gq, gk, bias_tab, sinks, *, batch, seq):
    nb = seq // BLOCK
    k_blk = 1920 // LANE
    v_blk = 2048 // LANE
    prev = lambda b, n: b * nb + jnp.maximum(n - 1, 0)
    return pl.pallas_call(
        _swa_kernel,
        out_shape=jax.ShapeDtypeStruct((batch * seq, H_B * HEAD_DIM), jnp.bfloat16),
        grid=(batch, nb),
        in_specs=[
            pl.BlockSpec((BLOCK, H_B * HEAD_DIM), lambda b, n: (b * nb + n, 0)),
            pl.BlockSpec((BLOCK, LANE), lambda b, n: (b * nb + n, k_blk)),
            pl.BlockSpec((BLOCK, LANE), lambda b, n: (prev(b, n), k_blk)),
            pl.BlockSpec((BLOCK, LANE), lambda b, n: (b * nb + n, v_blk)),
            pl.BlockSpec((BLOCK, LANE), lambda b, n: (prev(b, n), v_blk)),
            pl.BlockSpec((1, LANE), lambda b, n: (0, 0)),
            pl.BlockSpec((1, LANE), lambda b, n: (0, 0)),
            pl.BlockSpec(bias_tab.shape, lambda b, n: (0, 0, 0)),
            pl.BlockSpec(memory_space=pltpu.SMEM),
        ],
        out_specs=pl.BlockSpec((BLOCK, H_B * HEAD_DIM), lambda b, n: (b * nb + n, 0)),
        compiler_params=_cp(("parallel", "parallel")),
        name="swa_attn",
    )(proj, proj, proj, proj, proj, jnp.tile(gq, 2).reshape(1, LANE), jnp.tile(gk, 2).reshape(1, LANE),
      bias_tab, sinks)


def _top16(v):
    return pltpu.bitcast(pltpu.bitcast(v, jnp.int32) & jnp.int32(-65536), jnp.float32)


def _split3(c):
    hi = _top16(c)
    r = c - hi
    mid = _top16(r)
    return hi, mid, r - mid


def _head_inv_rms(x_ref, e_ref):
    ss = jnp.zeros((x_ref.shape[0], LANE), jnp.float32)
    for p in range(x_ref.shape[1] // LANE):
        x = x_ref[:, p * LANE:(p + 1) * LANE]
        x2 = x * x
        hi = _top16(x2)
        e_p = e_ref[p * LANE:(p + 1) * LANE, :]
        ss = ss + jnp.dot(hi.astype(jnp.bfloat16), e_p, preferred_element_type=jnp.float32)
        ss = ss + jnp.dot((x2 - hi).astype(jnp.bfloat16), e_p, preferred_element_type=jnp.float32)
    return jnp.concatenate([t.astype(jnp.bfloat16) for t in _split3(lax.rsqrt(ss / HEAD_DIM + EPS))], axis=1)


def _fox_prep_kernel(q_ref, k_ref, v_ref, f_ref, gq_ref, gk_ref, fb_ref, e_ref, et_ref, qo_ref, ko_ref, vt_ref,
                     carry_ref):
    tm = q_ref.shape[0]

    @pl.when(pl.program_id(1) == 0)
    def _():
        carry_ref[...] = jnp.zeros_like(carry_ref)

    z = f_ref[...] + fb_ref[...]
    logf = jnp.minimum(z, 0.0) - jnp.log1p(jnp.exp(-jnp.abs(z)))
    r = lax.broadcasted_iota(jnp.int32, (tm, tm), 0)
    c = lax.broadcasted_iota(jnp.int32, (tm, tm), 1)
    tri = jnp.where(c <= r, 1.0, 0.0)
    cum = jnp.dot(tri, logf, preferred_element_type=jnp.float32, precision=lax.Precision.HIGHEST) + carry_ref[...]
    carry_ref[...] = cum[tm - 1:tm, :]

    inv_q = _head_inv_rms(q_ref, e_ref)
    inv_k = _head_inv_rms(k_ref, e_ref)

    lane = lax.broadcasted_iota(jnp.int32, (tm, LANE), 1)
    masks = []
    for half in range(2):
        al = lane - (HEAD_DIM if half == 0 else 0)
        first, second = (al >= 0) & (al < 3), (al >= 3) & (al < 6)
        masks.append(((al == 0) | (al == 3), (al == 1) | (al == 4), first, second,
                      jnp.where(second, 1.0, 0.0), jnp.where(first, 1.0, 0.0),
                      (lane < HEAD_DIM) if half == 0 else (lane >= HEAD_DIM)))
    for p in range(H_C // 2):
        sl = slice(p * LANE, (p + 1) * LANE)
        qn = q_ref[:, sl] * jnp.dot(inv_q, et_ref[:, sl], preferred_element_type=jnp.float32) * gq_ref[:, sl]
        kn = k_ref[:, sl] * jnp.dot(inv_k, et_ref[:, sl], preferred_element_type=jnp.float32) * gk_ref[:, sl]
        for half in range(2):
            h = 2 * p + half
            is_hi, is_mid, first, second, ones_q, ones_k, own = masks[half]
            hi, mid, lo = _split3(jnp.broadcast_to(cum[:, h:h + 1], (tm, LANE)))
            part = jnp.where(is_hi, hi, jnp.where(is_mid, mid, lo))
            qo_ref[:, h * LANE:(h + 1) * LANE] = jnp.where(
                own, qn, jnp.where(first, part, ones_q)).astype(qo_ref.dtype)
            ko_ref[:, h * LANE:(h + 1) * LANE] = jnp.where(
                own, kn, jnp.where(second, -part, ones_k)).astype(ko_ref.dtype)
    ones = jnp.ones((V_ROWS - HEAD_DIM, tm), vt_ref.dtype)
    for p in range(H_C // 2):
        vt = v_ref[:, p * LANE:(p + 1) * LANE].T.astype(vt_ref.dtype)
        for half in range(2):
            h = 2 * p + half
            vt_ref[h * V_ROWS:h * V_ROWS + HEAD_DIM, :] = vt[half * HEAD_DIM:(half + 1) * HEAD_DIM, :]
            vt_ref[h * V_ROWS + HEAD_DIM:(h + 1) * V_ROWS, :] = ones


def fox_prep(proj, gq, gk, fbias, *, batch, seq, tm):
    m = proj.shape[0]
    hd = H_C * HEAD_DIM
    nt = seq // tm
    f_blk = 3 * hd // LANE
    fb = jnp.zeros((1, LANE), jnp.float32).at[0, :H_C].set(fbias)
    aug = jax.ShapeDtypeStruct((m, H_C * LANE), jnp.bfloat16)
    row = lambda b, i: b * nt + i
    e_np = (np.arange(hd)[:, None] // HEAD_DIM == np.arange(LANE)[None, :]).astype(np.float32)
    e = jnp.asarray(e_np, jnp.bfloat16)
    et = jnp.asarray(np.tile(e_np.T, (3, 1)), jnp.bfloat16)
    return pl.pallas_call(
        _fox_prep_kernel,
        out_shape=(aug, aug, jax.ShapeDtypeStruct((batch, H_C * V_ROWS, seq), jnp.bfloat16)),
        grid=(batch, nt),
        in_specs=[
            pl.BlockSpec((tm, hd), lambda b, i: (row(b, i), 0)),
            pl.BlockSpec((tm, hd), lambda b, i: (row(b, i), 1)),
            pl.BlockSpec((tm, hd), lambda b, i: (row(b, i), 2)),
            pl.BlockSpec((tm, LANE), lambda b, i: (row(b, i), f_blk)),
            pl.BlockSpec((1, hd), lambda b, i: (0, 0)),
            pl.BlockSpec((1, hd), lambda b, i: (0, 0)),
            pl.BlockSpec((1, LANE), lambda b, i: (0, 0)),
            pl.BlockSpec((hd, LANE), lambda b, i: (0, 0)),
            pl.BlockSpec((3 * LANE, hd), lambda b, i: (0, 0)),
        ],
        out_specs=(
            pl.BlockSpec((tm, H_C * LANE), lambda b, i: (row(b, i), 0)),
            pl.BlockSpec((tm, H_C * LANE), lambda b, i: (row(b, i), 0)),
            pl.BlockSpec((None, H_C * V_ROWS, tm), lambda b, i: (b, 0, i)),
        ),
        scratch_shapes=[pltpu.VMEM((1, LANE), jnp.float32)],
        compiler_params=_cp(("parallel", "arbitrary")),
        name="fox_prep",
    )(proj, proj, proj, proj, jnp.tile(gq * HEAD_DIM ** -0.5, H_C).reshape(1, hd), jnp.tile(gk, H_C).reshape(1, hd),
      fb, e, et)


def _fox_attn_kernel(tq, q_ref, k_ref, vt_ref, o_ref, s_ref, p_ref):
    seq = q_ref.shape[0]
    nq = seq // tq
    diag_ok = (lax.broadcasted_iota(jnp.int32, (tq, tq), 0) <= lax.broadcasted_iota(jnp.int32, (tq, tq), 1))

    def logits(i):
        nk = (i + 1) * tq
        for hh in range(2):
            hl = slice(hh * LANE, (hh + 1) * LANE)
            s = lax.dot_general(k_ref[0:nk, hl], q_ref[i * tq:nk, hl], _NT,
                                preferred_element_type=jnp.float32)
            if i > 0:
                s_ref[i % 2, hh, 0:i * tq, :] = s[:i * tq]
            s_ref[i % 2, hh, i * tq:nk, :] = jnp.where(diag_ok, s[i * tq:], NEG)

    def probs(i):
        nk = (i + 1) * tq
        for hh in range(2):
            s = s_ref[i % 2, hh, 0:nk, :]
            p_ref[i % 2, hh, 0:nk, :] = jnp.exp(s - jnp.max(s, axis=0, keepdims=True)).astype(p_ref.dtype)

    def values(i):
        nk = (i + 1) * tq
        outs = []
        for hh in range(2):
            acc = jnp.dot(vt_ref[hh * V_ROWS:(hh + 1) * V_ROWS, 0:nk], p_ref[i % 2, hh, 0:nk, :],
                          preferred_element_type=jnp.float32)
            outs.append(acc[:HEAD_DIM] / acc[HEAD_DIM:HEAD_DIM + 1])
        o_ref[i * tq:nk, :] = jnp.concatenate(outs, axis=0).T.astype(o_ref.dtype)

    logits(0)
    for i in range(nq):
        if i > 0:
            values(i - 1)
        if i + 1 < nq:
            logits(i + 1)
        probs(i)
    values(nq - 1)


def fox_attn(qa, ka, vt, *, batch, seq, tq):
    pairs = H_C // 2
    return pl.pallas_call(
        functools.partial(_fox_attn_kernel, tq),
        out_shape=jax.ShapeDtypeStruct((batch * seq, H_C * HEAD_DIM), jnp.bfloat16),
        grid=(batch, pairs),
        in_specs=[
            pl.BlockSpec((seq, 2 * LANE), lambda b, c: (b, c)),
            pl.BlockSpec((seq, 2 * LANE), lambda b, c: (b, c)),
            pl.BlockSpec((None, 2 * V_ROWS, seq), lambda b, c: (b, c, 0)),
        ],
        out_specs=pl.BlockSpec((seq, 2 * HEAD_DIM), lambda b, c: (b, c)),
        scratch_shapes=[pltpu.VMEM((2, 2, seq, tq), jnp.float32), pltpu.VMEM((2, 2, seq, tq), jnp.bfloat16)],
        compiler_params=_cp(("parallel", "parallel")),
        name="fox_attn",
    )(qa, ka, vt)


def _rel_bucket(dist):
    n = jnp.maximum(dist, 0)
    exact = N_BUCKETS // 2
    nf = jnp.maximum(n, 1).astype(jnp.float32)
    large = exact + (jnp.log(nf / exact) / math.log(MAX_DISTANCE / exact) * (N_BUCKETS - exact)).astype(jnp.int32)
    large = jnp.minimum(large, N_BUCKETS - 1)
    return jnp.where(n < exact, n, large)


def _toeplitz(v, rows, cols, off):
    h, dmax = v.shape
    lt = cols + rows - 1
    idx = off + rows - 1 - np.arange(lt)
    e = jnp.where(((idx >= 0) & (idx < dmax))[None], v[:, np.clip(idx, 0, dmax - 1)], 0.0)
    f = jnp.pad(e, ((0, 0), (0, 1)))
    a = jnp.tile(f, (1, rows))[:, :rows * lt].reshape(h, rows, lt)
    return a[:, :, rows - 1:rows - 1 + cols]


def kernel(x, p, attn_norm, ffn_norm, ple_norm, rel_bias, w_in_even, a_cq_norm, a_ckv_norm, a_w_uq, a_q_norm,
           a_w_qidx, a_w_uv, b_q_norm, b_k_norm, b_sinks, w_out_even, w_in_odd, c_forget_bias, c_q_norm,
           c_k_norm, w_out_odd, w_up, ffn_conv, w_down, w_ple_gate, w_ple_proj):
    batch, seq, d = x.shape
    depth = p.shape[0]
    m = batch * seq
    bf = jnp.bfloat16
    d_ff = w_down.shape[1]
    f_pad = -(-d_ff // 512) * 512
    hd_c = H_C * HEAD_DIM

    bias_by_dist = rel_bias[_rel_bucket(jnp.arange(seq))].T
    g_off = seq - BLOCK
    g_cols = g_off + KEY_TILE
    gtab = jnp.transpose(_toeplitz(bias_by_dist[:H_A], BLOCK, g_cols, g_off), (2, 0, 1)).reshape(g_cols, H_A * BLOCK)
    dist_b = np.arange(BLOCK)[:, None] + BLOCK - np.arange(2 * BLOCK)[None, :]
    band = (dist_b >= 0) & (dist_b < WINDOW)
    btab = jnp.where(band[None], _toeplitz(bias_by_dist[H_A:, :WINDOW], BLOCK, 2 * BLOCK, BLOCK), NEG)

    o1 = Q_LORA
    o2 = o1 + KV_LORA
    o3 = o2 + IDX_DIM
    o4 = o3 + IDX_HEADS
    o5 = o4 + H_B * HEAD_DIM
    o6 = o5 + KV_B * HEAD_DIM

    w_in_e = cast_cols(w_in_even, [(o4, o5 - o4, 0), (0, o1, 1024), (o1, o2 - o1, 1536), (o2, o4 - o2, 1792),
                                   (o5, o6 - o5, 1920), (o6, w_in_even.shape[2] - o6, 2048)], 2304, tr=512)
    w_in_o = cast_cols(w_in_odd, [(0, w_in_odd.shape[2], 0)], 3 * hd_c + LANE, tr=256)
    w_uq = cast_cols(a_w_uq, [(0, a_w_uq.shape[2], 0)], a_w_uq.shape[2], tr=256)
    w_qi = cast_cols(a_w_qidx, [(0, a_w_qidx.shape[2], 0)], a_w_qidx.shape[2], tr=512)
    w_out_e = cast_cols(w_out_even, [(0, d, 0)], d, tr=512)
    w_out_o = cast_cols(w_out_odd, [(0, d, 0)], d, tr=512)
    wgu = cast_cols(w_up, [(0, d_ff, 0), (d_ff, d_ff, f_pad)], 2 * f_pad, tr=128)
    cgu = jnp.pad(ffn_conv.reshape(-1, CONV_WIDTH, 2, d_ff),
                  ((0, 0), (0, 0), (0, 0), (0, f_pad - d_ff))).reshape(-1, CONV_WIDTH, 2 * f_pad)
    wd = cast_rows(w_down, f_pad, tr=LANE)
    w_pg = cast_cols(w_ple_gate, [(0, d, 0)], d, tr=512)
    w_pp = cast_cols(w_ple_proj, [(0, d, 0)], d, tr=w_ple_proj.shape[1])
    p = p.reshape(depth, m, -1)
    wuv = jnp.swapaxes(a_w_uv, 2, 3).reshape(-1, H_A // 2, 2, HEAD_DIM, KV_LORA)
    zero = jnp.zeros_like(wuv[:, :, 0])
    wuv2 = jnp.concatenate([jnp.concatenate([wuv[:, :, 0], zero], axis=3),
                            jnp.concatenate([zero, wuv[:, :, 1]], axis=3)], axis=2).astype(bf)

    x = x.reshape(m, d)
    for i in range(depth):
        if i % 2 == 0:
            e = i // 2
            proj = norm_matmul(x, attn_norm[i], w_in_e, e, tm=1024, tn=w_in_e.shape[2] // 3)
            q, qi, kv, kvt = dsa_prep(proj, a_cq_norm[e], a_ckv_norm[e], a_q_norm[e], w_uq, w_qi, e, seq=seq, tm=256)
            y_a = dsa_attn(q, qi, proj, kv, kvt, gtab, wuv2[e], batch=batch, seq=seq)
            y_b = swa_attn(proj, b_q_norm[e], b_k_norm[e], btab, b_sinks[e], batch=batch, seq=seq)
            x = matmul_res([y_a, y_b], w_out_e, e, x, tm=1024, tn=512)
        else:
            o = i // 2
            proj = norm_matmul(x, attn_norm[i], w_in_o, o, tm=1024, tn=w_in_o.shape[2] // 7)
            qa, ka, vt = fox_prep(proj, c_q_norm[o], c_k_norm[o], c_forget_bias[o], batch=batch, seq=seq, tm=256)
            y = fox_attn(qa, ka, vt, batch=batch, seq=seq, tq=256)
            x = matmul_res([y], w_out_o, o, x, tm=1024, tn=512)
        x = conv_ffn(x, ffn_norm[i], wgu, cgu, wd, i, seq=seq, tm=1024, tf=512)
        x = ple(x, ple_norm[i], p, w_pg, w_pp, i, tm=1024, tn=512)
    return x.reshape(batch, seq, d)
```
